```python
import jax, jax.numpy as jnp
from jax import lax
import numpy as np

D_MODEL = 1024
BATCH = 16
SEQ = 256
DEPTH = 4
DEC_BATCH = 8
DEC_SEQ = 1024
PAST_LEN = 512

GRID_W = 64
MIX_W = D_MODEL
CONV_W = MIX_W // 4
ATTN_W = MIX_W // 2
POOL_W = MIX_W - CONV_W - ATTN_W
HEAD_DIM = 64
N_HEADS = ATTN_W // HEAD_DIM
N_KV_HEADS = 2
GQA_GROUP = N_HEADS // N_KV_HEADS
KV_W = N_KV_HEADS * HEAD_DIM
WINDOW = 128
BLOCK = 128
CONV_K = 31
POOL_WINDOWS = (2, 4, 8, 16)
N_POOL_GROUPS = len(POOL_WINDOWS)
POOL_GROUP_W = POOL_W // N_POOL_GROUPS
ROPE_BASE = 10000.0
EPS = 1e-6
NEG_INF = -1e30

OFF_A_VAL = 0
OFF_A_GLU = OFF_A_VAL + CONV_W
OFF_A_GATE = OFF_A_GLU + CONV_W
OFF_Q = OFF_A_GATE + CONV_W
OFF_K = OFF_Q + ATTN_W
OFF_V = OFF_K + KV_W
OFF_B_GATE = OFF_V + KV_W
OFF_C_IN = OFF_B_GATE + ATTN_W
OFF_C_GATE = OFF_C_IN + POOL_W
IN_W = OFF_C_GATE + POOL_W

kernel_name = "hybrid_dit_conv_swa_pool_step"


def rms_norm(x, g):
    xf = x.astype(jnp.float32)
    xf = xf * lax.rsqrt(jnp.mean(xf * xf, axis=-1, keepdims=True) + EPS)
    return xf.astype(x.dtype) * g


def rope_2d(x, rows, cols):
    half = HEAD_DIM // 2
    quarter = half // 2
    freqs = ROPE_BASE ** (-jnp.arange(quarter, dtype=jnp.float32) / quarter)

    def rot(xp, pos):
        ang = pos.astype(jnp.float32)[:, None] * freqs[None, :]
        cos = jnp.cos(ang)[None, :, None, :]
        sin = jnp.sin(ang)[None, :, None, :]
        x1 = xp[..., :quarter].astype(jnp.float32)
        x2 = xp[..., quarter:].astype(jnp.float32)
        return jnp.concatenate([x1 * cos - x2 * sin, x1 * sin + x2 * cos], axis=-1)

    out = jnp.concatenate([rot(x[..., :half], rows), rot(x[..., half:], cols)], axis=-1)
    return out.astype(x.dtype)


def softmax_with_sink(s, sink_kg):
    sb = jnp.broadcast_to(sink_kg[None, :, :, None, None], s.shape[:-1] + (1,))
    p = jax.nn.softmax(jnp.concatenate([s, sb], axis=-1), axis=-1)
    return p[..., :-1]


def context_attention(q, k, v, sink):
    B, S = q.shape[0], q.shape[1]
    nq = S // BLOCK
    scale = HEAD_DIM ** -0.5
    sink_kg = sink.reshape(N_KV_HEADS, GQA_GROUP).astype(jnp.float32)
    qb = q.reshape(B, nq, BLOCK, N_KV_HEADS, GQA_GROUP, HEAD_DIM).transpose(1, 0, 2, 3, 4, 5)

    def one(qblk):
        s = jnp.einsum('bqkgd,btkd->bkgqt', qblk, k, preferred_element_type=jnp.float32) * scale
        p = softmax_with_sink(s, sink_kg)
        return jnp.einsum('bkgqt,btkd->bqkgd', p.astype(v.dtype), v)

    o = lax.map(one, qb)
    return o.transpose(1, 0, 2, 3, 4, 5).reshape(B, S, ATTN_W)


def windowed_attention(q, k, v, ck, cv, sink):
    B, L = q.shape[0], q.shape[1]
    nb = L // BLOCK
    scale = HEAD_DIM ** -0.5
    sink_kg = sink.reshape(N_KV_HEADS, GQA_GROUP).astype(jnp.float32)
    qb = q.reshape(B, nb, BLOCK, N_KV_HEADS, GQA_GROUP, HEAD_DIM).transpose(1, 0, 2, 3, 4, 5)
    pad = ((0, 0), (BLOCK, BLOCK), (0, 0), (0, 0))
    kp = jnp.pad(k, pad)
    vp = jnp.pad(v, pad)

    def band(t):
        parts = [t[:, j * BLOCK: j * BLOCK + L].reshape(B, nb, BLOCK, N_KV_HEADS, HEAD_DIM) for j in range(3)]
        return jnp.concatenate(parts, axis=2).transpose(1, 0, 2, 3, 4)

    kb, vb = band(kp), band(vp)
    blk = jnp.arange(nb)[:, None]
    qpos = blk * BLOCK + jnp.arange(BLOCK)[None, :]
    kpos = (blk - 1) * BLOCK + jnp.arange(3 * BLOCK)[None, :]
    rel = kpos[:, None, :] - qpos[:, :, None]
    valid = (jnp.abs(rel) <= WINDOW) & (kpos[:, None, :] >= 0) & (kpos[:, None, :] < L)

    def one(args):
        qblk, kblk, vblk, vmask = args
        s_loc = jnp.einsum('bqkgd,bjkd->bkgqj', qblk, kblk, preferred_element_type=jnp.float32) * scale
        s_loc = jnp.where(vmask[None, None, None], s_loc, NEG_INF)
        s_ctx = jnp.einsum('bqkgd,bpkd->bkgqp', qblk, ck, preferred_element_type=jnp.float32) * scale
        p = softmax_with_sink(jnp.concatenate([s_loc, s_ctx], axis=-1), sink_kg)
        p_loc = p[..., :3 * BLOCK].astype(v.dtype)
        p_ctx = p[..., 3 * BLOCK:].astype(v.dtype)
        return (jnp.einsum('bkgqj,bjkd->bqkgd', p_loc, vblk)
                + jnp.einsum('bkgqp,bpkd->bqkgd', p_ctx, cv))

    o = lax.map(one, (qb, kb, vb, valid))
    return o.transpose(1, 0, 2, 3, 4, 5).reshape(B, L, ATTN_W)


def conv_branch(a_val, a_glu, dw, db, ln_g, ln_b, w_pw):
    u = a_val * jax.nn.sigmoid(a_glu)
    y = lax.conv_general_dilated(u, dw[:, None, :], window_strides=(1,),
                                 padding=[(CONV_K // 2, CONV_K // 2)],
                                 dimension_numbers=('NWC', 'WIO', 'NWC'),
                                 feature_group_count=CONV_W) + db
    yf = y.astype(jnp.float32)
    mu = jnp.mean(yf, axis=-1, keepdims=True)
    var = jnp.mean(jnp.square(yf - mu), axis=-1, keepdims=True)
    yn = ((yf - mu) * lax.rsqrt(var + EPS)).astype(y.dtype) * ln_g + ln_b
    return jax.nn.silu(yn) @ w_pw


def pool_branch(u, pool_w, pool_scale):
    B, L, _ = u.shape
    cs = jnp.concatenate([jnp.zeros((B, 1, POOL_W), jnp.float32),
                          jnp.cumsum(u.astype(jnp.float32), axis=1)], axis=1)
    t = jnp.arange(L)
    outs = []
    for gi, w in enumerate(POOL_WINDOWS):
        lo = jnp.clip(t - w // 2, 0, L)
        hi = jnp.clip(t + w - w // 2, 0, L)
        c0, c1 = gi * POOL_GROUP_W, (gi + 1) * POOL_GROUP_W
        c = cs[..., c0:c1]
        mean = (jnp.take(c, hi, axis=1) - jnp.take(c, lo, axis=1)) / (hi - lo).astype(jnp.float32)[None, :, None]
        outs.append((mean.astype(u.dtype) - u[..., c0:c1]) @ pool_w[gi])
    return jnp.concatenate(outs, axis=-1) * pool_scale


def mixer_layer(x, mod, attend, norm_w, w_in, conv_dw, conv_b, conv_ln_g, conv_ln_b, conv_pw,
                pool_w, pool_scale, w_out):
    B, L = x.shape[0], x.shape[1]
    shift, scl, gate = jnp.split(mod, 3, axis=-1)
    h = rms_norm(x, norm_w) * (1 + scl) + shift
    u = h @ w_in
    a_out = conv_branch(u[..., OFF_A_VAL:OFF_A_GLU], u[..., OFF_A_GLU:OFF_A_GATE],
                        conv_dw, conv_b, conv_ln_g, conv_ln_b, conv_pw) * jax.nn.silu(u[..., OFF_A_GATE:OFF_Q])
    q = u[..., OFF_Q:OFF_K].reshape(B, L, N_HEADS, HEAD_DIM)
    k = u[..., OFF_K:OFF_V].reshape(B, L, N_KV_HEADS, HEAD_DIM)
    v = u[..., OFF_V:OFF_B_GATE].reshape(B, L, N_KV_HEADS, HEAD_DIM)
    b_out = attend(q, k, v) * jax.nn.silu(u[..., OFF_B_GATE:OFF_C_IN])
    c_out = pool_branch(u[..., OFF_C_IN:OFF_C_GATE], pool_w, pool_scale) * jax.nn.silu(u[..., OFF_C_GATE:IN_W])
    y = jnp.concatenate([a_out, b_out, c_out], axis=-1) @ w_out
    return x + gate * y, k, v


def setup_inputs(seed: int = 0) -> dict:
    key = jax.random.key(seed)
    ks = jax.random.split(key, 24)
    f32 = jnp.float32
    nrm = lambda k, shape, s: jax.random.normal(k, shape, f32) * s
    return {
        "x_prompt": nrm(ks[0], (BATCH, SEQ, D_MODEL), 1.0),
        "x_sample": nrm(ks[1], (DEC_BATCH, DEC_SEQ, D_MODEL), 1.0),
        "c": nrm(ks[2], (DEC_BATCH, D_MODEL), 1.0),
        "cache_k": nrm(ks[3], (DEC_BATCH, DEPTH, PAST_LEN, N_KV_HEADS, HEAD_DIM), 1.0),
        "cache_v": nrm(ks[4], (DEC_BATCH, DEPTH, PAST_LEN, N_KV_HEADS, HEAD_DIM), 1.0),
        "c_ctx": nrm(ks[5], (D_MODEL,), 1.0),
        "w_ada": nrm(ks[6], (DEPTH, D_MODEL, 3 * D_MODEL), 0.5 * D_MODEL ** -0.5),
        "b_ada": nrm(ks[7], (DEPTH, 3 * D_MODEL), 0.01),
        "norm_w": 1.0 + nrm(ks[8], (DEPTH, D_MODEL), 0.02),
        "w_in": nrm(ks[9], (DEPTH, D_MODEL, IN_W), D_MODEL ** -0.5),
        "conv_dw": nrm(ks[10], (DEPTH, CONV_K, CONV_W), CONV_K ** -0.5),
        "conv_b": nrm(ks[11], (DEPTH, CONV_W), 0.01),
        "conv_ln_g": 1.0 + nrm(ks[12], (DEPTH, CONV_W), 0.02),
        "conv_ln_b": nrm(ks[13], (DEPTH, CONV_W), 0.01),
        "conv_pw": nrm(ks[14], (DEPTH, CONV_W, CONV_W), CONV_W ** -0.5),
        "attn_sink": nrm(ks[15], (DEPTH, N_HEADS), 0.5),
        "pool_w": nrm(ks[16], (DEPTH, N_POOL_GROUPS, POOL_GROUP_W, POOL_GROUP_W), POOL_GROUP_W ** -0.5),
        "pool_scale": 1.0 + nrm(ks[17], (DEPTH, POOL_W), 0.02),
        "w_out": nrm(ks[18], (DEPTH, MIX_W, D_MODEL), MIX_W ** -0.5),
        "final_norm_w": 1.0 + nrm(ks[19], (D_MODEL,), 0.02),
    }


def reference(x_prompt, x_sample, c, cache_k, cache_v, c_ctx, w_ada, b_ada, norm_w, w_in,
              conv_dw, conv_b, conv_ln_g, conv_ln_b, conv_pw, attn_sink, pool_w, pool_scale,
              w_out, final_norm_w):
    xp = x_prompt
    new_k, new_v = [], []
    for l in range(DEPTH):
        mod = (jax.nn.silu(c_ctx) @ w_ada[l] + b_ada[l])[None, None, :]
        sink_l = attn_sink[l]
        attend = lambda q, k, v, s=sink_l: context_attention(q, k, v, s)
        xp, k_l, v_l = mixer_layer(xp, mod, attend, norm_w[l], w_in[l], conv_dw[l], conv_b[l],
                                   conv_ln_g[l], conv_ln_b[l], conv_pw[l], pool_w[l], pool_scale[l], w_out[l])
        new_k.append(k_l)
        new_v.append(v_l)
    y_prompt = rms_norm(xp, final_norm_w)
    new_cache_k = jnp.stack(new_k, axis=1)
    new_cache_v = jnp.stack(new_v, axis=1)

    L = x_sample.shape[1]
    n_rows = L // GRID_W
    rows = jnp.broadcast_to(jnp.arange(n_rows)[:, None], (n_rows, GRID_W)).reshape(L)
    cols = jnp.broadcast_to(jnp.arange(GRID_W)[None, :], (n_rows, GRID_W)).reshape(L)
    xs = x_sample
    for l in range(DEPTH):
        mod = (jax.nn.silu(c) @ w_ada[l] + b_ada[l])[:, None, :]
        sink_l, ck_l, cv_l = attn_sink[l], cache_k[:, l], cache_v[:, l]
        attend = lambda q, k, v, s=sink_l, ck=ck_l, cv=cv_l: windowed_attention(
            rope_2d(q, rows, cols), rope_2d(k, rows, cols), v, ck, cv, s)
        xs, _, _ = mixer_layer(xs, mod, attend, norm_w[l], w_in[l], conv_dw[l], conv_b[l],
                               conv_ln_g[l], conv_ln_b[l], conv_pw[l], pool_w[l], pool_scale[l], w_out[l])
    y_sample = rms_norm(xs, final_norm_w)
    return (y_prompt, y_sample, new_cache_k, new_cache_v)
```

```python
import functools

import jax
import jax.numpy as jnp
from jax import lax
from jax.experimental import pallas as pl
from jax.experimental.pallas import tpu as pltpu

D_MODEL = 1024
DEPTH = 4
GRID_W = 64
CONV_W = 256
ATTN_W = 512
POOL_W = 256
HEAD_DIM = 64
N_HEADS = 8
N_KV_HEADS = 2
GQA_GROUP = 4
KV_W = 128
WINDOW = 128
BLOCK = 128
CONV_K = 31
POOL_WINDOWS = (2, 4, 8, 16)
POOL_GROUP_W = 64
ROPE_BASE = 10000.0
EPS = 1e-6
NEG_INF = -1e30

OFF_A = 0
OFF_Q = 3 * CONV_W
OFF_KV = OFF_Q + ATTN_W
OFF_B_GATE = OFF_KV + 2 * KV_W
OFF_C = OFF_B_GATE + ATTN_W
IN_W = OFF_C + 2 * POOL_W

LANES = 128
CONV_PAD = 16
POOL_PAD = 8
ROW_CHUNK = 256
CONV_TILE = 32
MOD_ROWS = 16
VMEM_LIMIT_BYTES = 56 * 1024 * 1024

F32 = jnp.float32
BF16 = jnp.bfloat16


def _silu(x):
    return x * jax.nn.sigmoid(x)


def _lane_iota(shape):
    return lax.broadcasted_iota(jnp.int32, shape, 1)


def _dup_halves(x):
    r = pltpu.roll(x, HEAD_DIM, axis=1)
    lo = _lane_iota(x.shape) < HEAD_DIM
    return jnp.where(lo, x, r), jnp.where(lo, r, x)


def _rope(x, cos, sin):
    quarter = HEAD_DIM // 4
    fwd = pltpu.roll(x, LANES - quarter, axis=1)
    bwd = pltpu.roll(x, quarter, axis=1)
    first = (_lane_iota(x.shape) & (2 * quarter - 1)) < quarter
    return x * cos + jnp.where(first, fwd, bwd) * sin


def _rms(x):
    return x * lax.rsqrt(jnp.mean(x * x, axis=-1, keepdims=True) + EPS)


def _adaln_kernel(c_ref, w_ref, b_ref, o_ref):
    s = _silu(c_ref[...])
    o_ref[0] = jnp.dot(s, w_ref[0], precision=lax.Precision.HIGHEST,
                       preferred_element_type=F32) + b_ref[0]


def _adaln(cvecs, w_ada, b_ada):
    n_col = 3
    return pl.pallas_call(
        _adaln_kernel,
        grid=(DEPTH, n_col),
        in_specs=[
            pl.BlockSpec((MOD_ROWS, D_MODEL), lambda l, j: (0, 0)),
            pl.BlockSpec((1, D_MODEL, D_MODEL), lambda l, j: (l, 0, j)),
            pl.BlockSpec((1, 1, D_MODEL), lambda l, j: (l, 0, j)),
        ],
        out_specs=pl.BlockSpec((1, MOD_ROWS, D_MODEL), lambda l, j: (l, 0, j)),
        out_shape=jax.ShapeDtypeStruct((DEPTH, MOD_ROWS, 3 * D_MODEL), F32),
        compiler_params=pltpu.CompilerParams(
            dimension_semantics=("arbitrary", "arbitrary"),
            vmem_limit_bytes=VMEM_LIMIT_BYTES),
        name="adaln_mod",
    )(cvecs, w_ada, b_ada.reshape(DEPTH, 1, 3 * D_MODEL))


def _layers_kernel(*refs, L, latent):
    if latent:
        (sink_ref, x_ref, mod_ref, nw_ref, win_ref, dw_ref, vec_ref, pw_ref, pbd_ref, wout_ref,
         fnw_ref, cos_ref, sin_ref, ck_ref, cv_ref,
         o_ref,
         xs_ref, gpad_ref, cpad_ref, gate_ref, qm_ref, kk_ref, vv_ref, mix_ref, cwin_ref, pwin_ref,
         ckk_ref, cvv_ref) = refs
    else:
        (sink_ref, x_ref, mod_ref, nw_ref, win_ref, dw_ref, vec_ref, pw_ref, pbd_ref, wout_ref,
         fnw_ref,
         o_ref, nk_ref, nv_ref,
         xs_ref, gpad_ref, cpad_ref, gate_ref, qm_ref, kk_ref, vv_ref, mix_ref, cwin_ref, pwin_ref) = refs

    l = pl.program_id(1)
    n_chunks = L // ROW_CHUNK
    kpad = BLOCK if latent else 0

    @pl.when(l == 0)
    def _():
        xs_ref[...] = x_ref[0]

    gpad_ref[0:CONV_PAD, :] = jnp.zeros((CONV_PAD, CONV_W), F32)
    gpad_ref[CONV_PAD + L:CONV_PAD + L + CONV_PAD, :] = jnp.zeros((CONV_PAD, CONV_W), F32)
    cpad_ref[0:POOL_PAD, :] = jnp.zeros((POOL_PAD, POOL_W), F32)
    cpad_ref[POOL_PAD + L:POOL_PAD + L + POOL_PAD, :] = jnp.zeros((POOL_PAD, POOL_W), F32)
    if latent:
        for g in range(N_KV_HEADS):
            for ref in (kk_ref, vv_ref):
                ref[g, 0:kpad, :] = jnp.zeros((kpad, LANES), BF16)
                ref[g, kpad + L:kpad + L + kpad, :] = jnp.zeros((kpad, LANES), BF16)
        ck0, ck1 = _dup_halves(ck_ref[0, 0])
        cv0, cv1 = _dup_halves(cv_ref[0, 0])
        ckk_ref[0] = ck0.astype(BF16)
        ckk_ref[1] = ck1.astype(BF16)
        cvv_ref[0] = cv0.astype(BF16)
        cvv_ref[1] = cv1.astype(BF16)

    shift = mod_ref[0, 0, 0:1, :]
    scl = mod_ref[0, 0, 1:2, :]
    gate = mod_ref[0, 0, 2:3, :]

    def proj_chunk(i, carry):
        r0 = pl.multiple_of(i * ROW_CHUNK, ROW_CHUNK)
        rows = pl.ds(r0, ROW_CHUNK)
        x = xs_ref[rows, :]
        h = (_rms(x) * nw_ref[0]) * (1.0 + scl) + shift
        hb = h.astype(BF16)

        ua = jnp.dot(hb, win_ref[0, :, OFF_A:OFF_Q], preferred_element_type=F32)
        glu = ua[:, 0:CONV_W] * jax.nn.sigmoid(ua[:, CONV_W:2 * CONV_W])
        gpad_ref[pl.ds(r0 + CONV_PAD, ROW_CHUNK), :] = glu
        gate_ref[rows, 0:CONV_W] = _silu(ua[:, 2 * CONV_W:3 * CONV_W])

        uq = jnp.dot(hb, win_ref[0, :, OFF_Q:OFF_KV], preferred_element_type=F32)
        ukv = jnp.dot(hb, win_ref[0, :, OFF_KV:OFF_B_GATE], preferred_element_type=F32)
        k = ukv[:, 0:KV_W]
        v = ukv[:, KV_W:2 * KV_W]
        if latent:
            cos = cos_ref[rows, :]
            sin = sin_ref[rows, :]
            k = _rope(k, cos, sin)
        else:
            nk_ref[0, 0, rows, :] = k
            nv_ref[0, 0, rows, :] = v
        lo = _lane_iota((ROW_CHUNK, LANES)) < HEAD_DIM
        scale = HEAD_DIM ** -0.5
        for p in range(ATTN_W // LANES):
            qp = uq[:, p * LANES:(p + 1) * LANES]
            if latent:
                qp = _rope(qp, cos, sin)
            qp = qp * scale
            qm_ref[rows, (2 * p) * LANES:(2 * p + 1) * LANES] = jnp.where(lo, qp, 0.0).astype(BF16)
            qm_ref[rows, (2 * p + 1) * LANES:(2 * p + 2) * LANES] = jnp.where(lo, 0.0, qp).astype(BF16)
        k0, k1 = _dup_halves(k)
        v0, v1 = _dup_halves(v)
        krows = pl.ds(r0 + kpad, ROW_CHUNK)
        kk_ref[0, krows, :] = k0.astype(BF16)
        kk_ref[1, krows, :] = k1.astype(BF16)
        vv_ref[0, krows, :] = v0.astype(BF16)
        vv_ref[1, krows, :] = v1.astype(BF16)

        ub = jnp.dot(hb, win_ref[0, :, OFF_B_GATE:OFF_C], preferred_element_type=F32)
        gate_ref[rows, CONV_W:CONV_W + ATTN_W] = _silu(ub)

        uc = jnp.dot(hb, win_ref[0, :, OFF_C:IN_W], preferred_element_type=F32)
        cpad_ref[pl.ds(r0 + POOL_PAD, ROW_CHUNK), :] = uc[:, 0:POOL_W]
        gate_ref[rows, CONV_W + ATTN_W:D_MODEL] = _silu(uc[:, POOL_W:2 * POOL_W])
        return carry

    lax.fori_loop(0, n_chunks, proj_chunk, 0)

    conv_b = vec_ref[0, 0:1, :]
    ln_g = vec_ref[0, 1:2, :]
    ln_b = vec_ref[0, 2:3, :]
    pool_scale = vec_ref[0, 3:4, :]

    def conv_chunk(i, carry):
        r0 = pl.multiple_of(i * ROW_CHUNK, ROW_CHUNK)
        rows = pl.ds(r0, ROW_CHUNK)
        pieces = []
        for t in range(ROW_CHUNK // CONV_TILE):
            cwin_ref[...] = gpad_ref[pl.ds(r0 + t * CONV_TILE, CONV_TILE + 2 * CONV_PAD), :]
            first = CONV_PAD - CONV_K // 2
            acc = jnp.zeros((CONV_TILE, CONV_W), F32)
            for kk in range(CONV_K):
                acc = acc + cwin_ref[first + kk:first + kk + CONV_TILE, :] * dw_ref[0, kk:kk + 1, :]
            y = acc + conv_b
            mu = jnp.mean(y, axis=-1, keepdims=True)
            yc = y - mu
            var = jnp.mean(yc * yc, axis=-1, keepdims=True)
            yn = (yc * lax.rsqrt(var + EPS)) * ln_g + ln_b
            pieces.append(_silu(yn).astype(BF16))
        z = jnp.concatenate(pieces, axis=0)
        a = jnp.dot(z, pw_ref[0], preferred_element_type=F32)
        mix_ref[rows, 0:CONV_W] = (a * gate_ref[rows, 0:CONV_W]).astype(BF16)
        return carry

    lax.fori_loop(0, n_chunks, conv_chunk, 0)

    n_local = 3 * BLOCK if latent else L
    lo_q = _lane_iota((BLOCK, LANES)) < HEAD_DIM

    def attn_block(i, carry):
        q0 = pl.multiple_of(i * BLOCK, BLOCK)
        qrows = pl.ds(q0, BLOCK)
        if latent:
            jq = lax.broadcasted_iota(jnp.int32, (BLOCK, n_local), 0)
            j3 = lax.broadcasted_iota(jnp.int32, (BLOCK, n_local), 1)
            rel = j3 - BLOCK - jq
            kpos = q0 - BLOCK + j3
            valid = (rel >= -WINDOW) & (rel <= WINDOW) & (kpos >= 0) & (kpos < L)
            local_rows = pl.ds(q0, n_local)
        else:
            local_rows = pl.ds(0, n_local)
        for g in range(N_KV_HEADS):
            qm = jnp.concatenate(
                [qm_ref[qrows, (g * GQA_GROUP + j) * LANES:(g * GQA_GROUP + j + 1) * LANES]
                 for j in range(GQA_GROUP)], axis=0)
            contract_last = (((1,), (1,)), ((), ()))
            s_loc = lax.dot_general(qm, kk_ref[g, local_rows, :], contract_last,
                                    preferred_element_type=F32)
            if latent:
                s_ctx = lax.dot_general(qm, ckk_ref[g], contract_last,
                                        preferred_element_type=F32)
            p_loc, p_ctx, inv = [], [], []
            for j in range(GQA_GROUP):
                hrows = slice(j * BLOCK, (j + 1) * BLOCK)
                sink = sink_ref[l, g * GQA_GROUP + j]
                sl = s_loc[hrows]
                if latent:
                    sl = jnp.where(valid, sl, NEG_INF)
                    sc = s_ctx[hrows]
                m = jnp.maximum(jnp.max(sl, axis=-1, keepdims=True), sink)
                if latent:
                    m = jnp.maximum(m, jnp.max(sc, axis=-1, keepdims=True))
                el = jnp.exp(sl - m)
                denom = jnp.sum(el, axis=-1, keepdims=True) + jnp.exp(sink - m)
                p_loc.append(el.astype(BF16))
                if latent:
                    ec = jnp.exp(sc - m)
                    denom = denom + jnp.sum(ec, axis=-1, keepdims=True)
                    p_ctx.append(ec.astype(BF16))
                inv.append(1.0 / denom)
            o = jnp.dot(jnp.concatenate(p_loc, axis=0), vv_ref[g, local_rows, :],
                        preferred_element_type=F32)
            if latent:
                o = o + jnp.dot(jnp.concatenate(p_ctx, axis=0), cvv_ref[g],
                                preferred_element_type=F32)
            for p in range(GQA_GROUP // 2):
                o_lo = o[(2 * p) * BLOCK:(2 * p + 1) * BLOCK] * inv[2 * p]
                o_hi = o[(2 * p + 1) * BLOCK:(2 * p + 2) * BLOCK] * inv[2 * p + 1]
                c0 = CONV_W + (g * (GQA_GROUP // 2) + p) * LANES
                b_out = jnp.where(lo_q, o_lo, o_hi)
                mix_ref[qrows, c0:c0 + LANES] = (b_out * gate_ref[qrows, c0:c0 + LANES]).astype(BF16)
        return carry

    lax.fori_loop(0, L // BLOCK, attn_block, 0)

    def pool_chunk(i, carry):
        r0 = pl.multiple_of(i * ROW_CHUNK, ROW_CHUNK)
        rows = pl.ds(r0, ROW_CHUNK)

        pwin_ref[...] = cpad_ref[pl.ds(r0, ROW_CHUNK + 2 * POOL_PAD), :]

        def tok(off):
            return pwin_ref[POOL_PAD + off:POOL_PAD + off + ROW_CHUNK, :]

        centre = tok(0)
        sums = []
        acc = None
        half_prev = 0
        for w in POOL_WINDOWS:
            half = w // 2
            for off in list(range(-half, -half_prev)) + list(range(half_prev, half)):
                term = centre if off == 0 else tok(off)
                acc = term if acc is None else acc + term
            sums.append(acc)
            half_prev = half
        lane = _lane_iota((ROW_CHUNK, POOL_W))
        t = r0 + lax.broadcasted_iota(jnp.int32, (ROW_CHUNK, POOL_W), 0)
        group = jnp.right_shift(lane, POOL_GROUP_W.bit_length() - 1)
        half_w = jnp.left_shift(1, group)
        cnt = jnp.minimum(t + half_w, L) - jnp.maximum(t - half_w, 0)
        total = jnp.where(group == 0, sums[0],
                          jnp.where(group == 1, sums[1], jnp.where(group == 2, sums[2], sums[3])))
        mean = total / cnt.astype(F32)
        d = (mean - centre).astype(BF16)
        c = jnp.dot(d, pbd_ref[0], preferred_element_type=F32) * pool_scale
        c0 = CONV_W + ATTN_W
        mix_ref[rows, c0:D_MODEL] = (c * gate_ref[rows, c0:D_MODEL]).astype(BF16)
        return carry

    lax.fori_loop(0, n_chunks, pool_chunk, 0)

    def out_chunk(i, carry):
        r0 = pl.multiple_of(i * ROW_CHUNK, ROW_CHUNK)
        rows = pl.ds(r0, ROW_CHUNK)
        y = jnp.dot(mix_ref[rows, :], wout_ref[0], preferred_element_type=F32)
        xs_ref[rows, :] = xs_ref[rows, :] + gate * y
        return carry

    lax.fori_loop(0, n_chunks, out_chunk, 0)

    @pl.when(l == DEPTH - 1)
    def _():
        def final_chunk(i, carry):
            r0 = pl.multiple_of(i * ROW_CHUNK, ROW_CHUNK)
            rows = pl.ds(r0, ROW_CHUNK)
            o_ref[0, rows, :] = _rms(xs_ref[rows, :]) * fnw_ref[...]
            return carry

        lax.fori_loop(0, n_chunks, final_chunk, 0)


def _mixer_layers(x, mods, mod_row, weights, rope=None, cache=None):
    S, L, _ = x.shape
    latent = cache is not None
    sink, norm_w, w_in, conv_dw, vecs, conv_pw, pool_bd, w_out, final_norm_w = weights
    kpad = BLOCK if latent else 0

    per_layer = lambda *tail: (lambda s, l: (l,) + tail)
    in_specs = [
        pl.BlockSpec(memory_space=pltpu.SMEM),
        pl.BlockSpec((1, L, D_MODEL), lambda s, l: (s, 0, 0)),
        pl.BlockSpec((1, 1, 3, D_MODEL), lambda s, l: (l, mod_row(s), 0, 0)),
        pl.BlockSpec((1, 1, D_MODEL), per_layer(0, 0)),
        pl.BlockSpec((1, D_MODEL, IN_W), per_layer(0, 0)),
        pl.BlockSpec((1, 32, CONV_W), per_layer(0, 0)),
        pl.BlockSpec((1, 8, CONV_W), per_layer(0, 0)),
        pl.BlockSpec((1, CONV_W, CONV_W), per_layer(0, 0)),
        pl.BlockSpec((1, POOL_W, POOL_W), per_layer(0, 0)),
        pl.BlockSpec((1, D_MODEL, D_MODEL), per_layer(0, 0)),
        pl.BlockSpec((1, D_MODEL), lambda s, l: (0, 0)),
    ]
    args = [sink, x, mods, norm_w, w_in, conv_dw, vecs, conv_pw, pool_bd, w_out, final_norm_w]
    out_specs = [pl.BlockSpec((1, L, D_MODEL), lambda s, l: (s, 0, 0))]
    out_shape = [jax.ShapeDtypeStruct((S, L, D_MODEL), F32)]
    scratch = [
        pltpu.VMEM((L, D_MODEL), F32),
        pltpu.VMEM((L + 2 * CONV_PAD, CONV_W), F32),
        pltpu.VMEM((L + 2 * POOL_PAD, POOL_W), F32),
        pltpu.VMEM((L, D_MODEL), F32),
        pltpu.VMEM((L, 2 * ATTN_W), BF16),
        pltpu.VMEM((N_KV_HEADS, L + 2 * kpad, LANES), BF16),
        pltpu.VMEM((N_KV_HEADS, L + 2 * kpad, LANES), BF16),
        pltpu.VMEM((L, D_MODEL), BF16),
        pltpu.VMEM((CONV_TILE + 2 * CONV_PAD, CONV_W), F32),
        pltpu.VMEM((ROW_CHUNK + 2 * POOL_PAD, POOL_W), F32),
    ]
    if latent:
        cos, sin = rope
        ck, cv = cache
        past = ck.shape[2]
        in_specs += [
            pl.BlockSpec((L, LANES), lambda s, l: (0, 0)),
            pl.BlockSpec((L, LANES), lambda s, l: (0, 0)),
            pl.BlockSpec((1, 1, past, KV_W), lambda s, l: (s, l, 0, 0)),
            pl.BlockSpec((1, 1, past, KV_W), lambda s, l: (s, l, 0, 0)),
        ]
        args += [cos, sin, ck, cv]
        scratch += [pltpu.VMEM((N_KV_HEADS, past, LANES), BF16),
                    pltpu.VMEM((N_KV_HEADS, past, LANES), BF16)]
    else:
        out_specs += [pl.BlockSpec((1, 1, L, KV_W), lambda s, l: (s, l, 0, 0))] * 2
        out_shape += [jax.ShapeDtypeStruct((S, DEPTH, L, KV_W), F32)] * 2

    return pl.pallas_call(
        functools.partial(_layers_kernel, L=L, latent=latent),
        grid=(S, DEPTH),
        in_specs=in_specs,
        out_specs=out_specs,
        out_shape=out_shape,
        scratch_shapes=scratch,
        compiler_params=pltpu.CompilerParams(
            dimension_semantics=("arbitrary", "arbitrary"),
            vmem_limit_bytes=VMEM_LIMIT_BYTES),
        name="latent_layers" if latent else "context_layers",
    )(*args)


def _rope_tables(L):
    quarter = HEAD_DIM // 4
    t = jnp.arange(L)
    rows = (t // GRID_W).astype(F32)
    cols = (t % GRID_W).astype(F32)
    freqs = ROPE_BASE ** (-jnp.arange(quarter, dtype=F32) / quarter)
    ang_r = rows[:, None] * freqs[None, :]
    ang_c = cols[:, None] * freqs[None, :]
    cos = jnp.concatenate([jnp.cos(ang_r)] * 2 + [jnp.cos(ang_c)] * 2, axis=-1)
    sin = jnp.concatenate([-jnp.sin(ang_r), jnp.sin(ang_r), -jnp.sin(ang_c), jnp.sin(ang_c)], axis=-1)
    return jnp.tile(cos, (1, 2)), jnp.tile(sin, (1, 2))


def kernel(x_prompt, x_sample, c, cache_k, cache_v, c_ctx, w_ada, b_ada, norm_w, w_in, conv_dw, conv_b,
           conv_ln_g, conv_ln_b, conv_pw, attn_sink, pool_w, pool_scale, w_out, final_norm_w):
    n_ctx, ctx_len, _ = x_prompt.shape
    n_lat, lat_len, _ = x_sample.shape
    past = cache_k.shape[2]
    assert n_lat + 1 <= MOD_ROWS

    cvecs = jnp.zeros((MOD_ROWS, D_MODEL), F32).at[:n_lat].set(c).at[n_lat].set(c_ctx)
    mods = _adaln(cvecs, w_ada, b_ada).reshape(DEPTH, MOD_ROWS, 3, D_MODEL)

    n_groups = len(POOL_WINDOWS)
    eye = jnp.eye(n_groups, dtype=F32)
    pool_bd = (pool_w[:, :, :, None, :] * eye[None, :, None, :, None]).reshape(DEPTH, POOL_W, POOL_W)
    vecs = jnp.zeros((DEPTH, 8, CONV_W), F32)
    vecs = vecs.at[:, 0].set(conv_b).at[:, 1].set(conv_ln_g).at[:, 2].set(conv_ln_b).at[:, 3].set(pool_scale)
    weights = (
        attn_sink,
        norm_w.reshape(DEPTH, 1, D_MODEL),
        w_in.astype(BF16),
        jnp.pad(conv_dw, ((0, 0), (0, 32 - CONV_K), (0, 0))),
        vecs,
        conv_pw.astype(BF16),
        pool_bd.astype(BF16),
        w_out.astype(BF16),
        final_norm_w.reshape(1, D_MODEL),
    )

    y_prompt, new_k, new_v = _mixer_layers(x_prompt, mods, lambda s: n_lat, weights)
    new_cache_k = new_k.reshape(n_ctx, DEPTH, ctx_len, N_KV_HEADS, HEAD_DIM)
    new_cache_v = new_v.reshape(n_ctx, DEPTH, ctx_len, N_KV_HEADS, HEAD_DIM)

    (y_sample,) = _mixer_layers(
        x_sample, mods, lambda s: s, weights,
        rope=_rope_tables(lat_len),
        cache=(cache_k.reshape(n_lat, DEPTH, past, KV_W), cache_v.reshape(n_lat, DEPTH, past, KV_W)))
    return (y_prompt, y_sample, new_cache_k, new_cache_v)
```

```python
import functools

import jax
import jax.numpy as jnp
from jax import lax
from jax.experimental import pallas as pl
from jax.experimental.pallas import tpu as pltpu

D_MODEL = 1024
DEPTH = 4
GRID_W = 64
CONV_W = 256
ATTN_W = 512
POOL_W = 256
HEAD_DIM = 64
N_HEADS = 8
N_KV_HEADS = 2
GQA_GROUP = 4
KV_W = 128
WINDOW = 128
BLOCK = 128
CONV_K = 31
POOL_WINDOWS = (2, 4, 8, 16)
POOL_GROUP_W = 64
ROPE_BASE = 10000.0
EPS = 1e-6
NEG_INF = -1e30

OFF_A = 0
OFF_Q = 3 * CONV_W
OFF_KV = OFF_Q + ATTN_W
OFF_B_GATE = OFF_KV + 2 * KV_W
OFF_C = OFF_B_GATE + ATTN_W
IN_W = OFF_C + 2 * POOL_W

LANES = 128
CONV_PAD = 16
POOL_PAD = 8
ROW_CHUNK = 256
CONV_TILE = 32
MOD_ROWS = 16
VMEM_LIMIT_BYTES = 56 * 1024 * 1024

F32 = jnp.float32
BF16 = jnp.bfloat16


def _silu(x):
    return x * jax.nn.sigmoid(x)


def _lane_iota(shape):
    return lax.broadcasted_iota(jnp.int32, shape, 1)


def _dup_halves(x):
    r = pltpu.roll(x, HEAD_DIM, axis=1)
    lo = _lane_iota(x.shape) < HEAD_DIM
    return jnp.where(lo, x, r), jnp.where(lo, r, x)


def _rope(x, cos, sin):
    quarter = HEAD_DIM // 4
    fwd = pltpu.roll(x, LANES - quarter, axis=1)
    bwd = pltpu.roll(x, quarter, axis=1)
    first = (_lane_iota(x.shape) & (2 * quarter - 1)) < quarter
    return x * cos + jnp.where(first, fwd, bwd) * sin


def _rms(x):
    return x * lax.rsqrt(jnp.mean(x * x, axis=-1, keepdims=True) + EPS)


def _adaln_kernel(c_ref, w_ref, b_ref, o_ref):
    s = _silu(c_ref[...])
    o_ref[0] = jnp.dot(s, w_ref[0], precision=lax.Precision.HIGHEST,
                       preferred_element_type=F32) + b_ref[0]


def _adaln(cvecs, w_ada, b_ada):
    n_col = 3
    return pl.pallas_call(
        _adaln_kernel,
        grid=(DEPTH, n_col),
        in_specs=[
            pl.BlockSpec((MOD_ROWS, D_MODEL), lambda l, j: (0, 0)),
            pl.BlockSpec((1, D_MODEL, D_MODEL), lambda l, j: (l, 0, j)),
            pl.BlockSpec((1, 1, D_MODEL), lambda l, j: (l, 0, j)),
        ],
        out_specs=pl.BlockSpec((1, MOD_ROWS, D_MODEL), lambda l, j: (l, 0, j)),
        out_shape=jax.ShapeDtypeStruct((DEPTH, MOD_ROWS, 3 * D_MODEL), F32),
        compiler_params=pltpu.CompilerParams(
            dimension_semantics=("arbitrary", "arbitrary"),
            vmem_limit_bytes=VMEM_LIMIT_BYTES),
        name="adaln_mod",
    )(cvecs, w_ada, b_ada.reshape(DEPTH, 1, 3 * D_MODEL))


def _layers_kernel(*refs, L, latent):
    if latent:
        (sink_ref, x_ref, mod_ref, nw_ref, win_ref, dw_ref, vec_ref, pw_ref, pbd_ref, wout_ref,
         fnw_ref, cos_ref, sin_ref, ck_ref, cv_ref,
         o_ref,
         xs_ref, gpad_ref, cpad_ref, gate_ref, qm_ref, kk_ref, vv_ref, mix_ref,
         ckk_ref, cvv_ref) = refs
    else:
        (sink_ref, x_ref, mod_ref, nw_ref, win_ref, dw_ref, vec_ref, pw_ref, pbd_ref, wout_ref,
         fnw_ref,
         o_ref, nk_ref, nv_ref,
         xs_ref, gpad_ref, cpad_ref, gate_ref, qm_ref, kk_ref, vv_ref, mix_ref) = refs

    l = pl.program_id(1)
    n_chunks = L // ROW_CHUNK
    kpad = BLOCK if latent else 0

    @pl.when(l == 0)
    def _():
        xs_ref[...] = x_ref[0]

    for c in range(CONV_W // LANES):
        gpad_ref[c, 0:CONV_PAD, :] = jnp.zeros((CONV_PAD, LANES), F32)
        gpad_ref[c, CONV_PAD + L:CONV_PAD + L + CONV_PAD, :] = jnp.zeros((CONV_PAD, LANES), F32)
        cpad_ref[c, 0:POOL_PAD, :] = jnp.zeros((POOL_PAD, LANES), F32)
        cpad_ref[c, POOL_PAD + L:POOL_PAD + L + POOL_PAD, :] = jnp.zeros((POOL_PAD, LANES), F32)
    if latent:
        for g in range(N_KV_HEADS):
            for ref in (kk_ref, vv_ref):
                ref[g, 0:kpad, :] = jnp.zeros((kpad, LANES), BF16)
                ref[g, kpad + L:kpad + L + kpad, :] = jnp.zeros((kpad, LANES), BF16)
        ck0, ck1 = _dup_halves(ck_ref[0, 0])
        cv0, cv1 = _dup_halves(cv_ref[0, 0])
        ckk_ref[0] = ck0.astype(BF16)
        ckk_ref[1] = ck1.astype(BF16)
        cvv_ref[0] = cv0.astype(BF16)
        cvv_ref[1] = cv1.astype(BF16)

    shift = mod_ref[0, 0, 0:1, :]
    scl = mod_ref[0, 0, 1:2, :]
    gate = mod_ref[0, 0, 2:3, :]

    def proj_chunk(i, carry):
        r0 = pl.multiple_of(i * ROW_CHUNK, ROW_CHUNK)
        rows = pl.ds(r0, ROW_CHUNK)
        x = xs_ref[rows, :]
        h = (_rms(x) * nw_ref[0]) * (1.0 + scl) + shift
        hb = h.astype(BF16)

        ua = jnp.dot(hb, win_ref[0, :, OFF_A:OFF_Q], preferred_element_type=F32)
        glu = ua[:, 0:CONV_W] * jax.nn.sigmoid(ua[:, CONV_W:2 * CONV_W])
        for c in range(CONV_W // LANES):
            gpad_ref[c, pl.ds(r0 + CONV_PAD, ROW_CHUNK), :] = glu[:, c * LANES:(c + 1) * LANES]
        gate_ref[rows, 0:CONV_W] = _silu(ua[:, 2 * CONV_W:3 * CONV_W])

        uq = jnp.dot(hb, win_ref[0, :, OFF_Q:OFF_KV], preferred_element_type=F32)
        ukv = jnp.dot(hb, win_ref[0, :, OFF_KV:OFF_B_GATE], preferred_element_type=F32)
        k = ukv[:, 0:KV_W]
        v = ukv[:, KV_W:2 * KV_W]
        if latent:
            cos = cos_ref[rows, :]
            sin = sin_ref[rows, :]
            k = _rope(k, cos, sin)
        else:
            nk_ref[0, 0, rows, :] = k
            nv_ref[0, 0, rows, :] = v
        lo = _lane_iota((ROW_CHUNK, LANES)) < HEAD_DIM
        scale = HEAD_DIM ** -0.5
        for p in range(ATTN_W // LANES):
            qp = uq[:, p * LANES:(p + 1) * LANES]
            if latent:
                qp = _rope(qp, cos, sin)
            qp = qp * scale
            qm_ref[rows, (2 * p) * LANES:(2 * p + 1) * LANES] = jnp.where(lo, qp, 0.0).astype(BF16)
            qm_ref[rows, (2 * p + 1) * LANES:(2 * p + 2) * LANES] = jnp.where(lo, 0.0, qp).astype(BF16)
        k0, k1 = _dup_halves(k)
        v0, v1 = _dup_halves(v)
        krows = pl.ds(r0 + kpad, ROW_CHUNK)
        kk_ref[0, krows, :] = k0.astype(BF16)
        kk_ref[1, krows, :] = k1.astype(BF16)
        vv_ref[0, krows, :] = v0.astype(BF16)
        vv_ref[1, krows, :] = v1.astype(BF16)

        ub = jnp.dot(hb, win_ref[0, :, OFF_B_GATE:OFF_C], preferred_element_type=F32)
        gate_ref[rows, CONV_W:CONV_W + ATTN_W] = _silu(ub)

        uc = jnp.dot(hb, win_ref[0, :, OFF_C:IN_W], preferred_element_type=F32)
        for c in range(POOL_W // LANES):
            cpad_ref[c, pl.ds(r0 + POOL_PAD, ROW_CHUNK), :] = uc[:, c * LANES:(c + 1) * LANES]
        gate_ref[rows, CONV_W + ATTN_W:D_MODEL] = _silu(uc[:, POOL_W:2 * POOL_W])
        return carry

    lax.fori_loop(0, n_chunks, proj_chunk, 0)

    conv_b = vec_ref[0, 0:1, :]
    ln_g = vec_ref[0, 1:2, :]
    ln_b = vec_ref[0, 2:3, :]
    pool_scale = vec_ref[0, 3:4, :]

    def conv_chunk(i, carry):
        r0 = pl.multiple_of(i * ROW_CHUNK, ROW_CHUNK)
        rows = pl.ds(r0, ROW_CHUNK)
        pieces = []
        for t in range(ROW_CHUNK // CONV_TILE):
            base = r0 + t * CONV_TILE + (CONV_PAD - CONV_K // 2)
            halves = []
            for c in range(CONV_W // LANES):
                acc = jnp.zeros((CONV_TILE, LANES), F32)
                for kk in range(CONV_K):
                    acc = acc + (gpad_ref[c, pl.ds(base + kk, CONV_TILE), :]
                                 * dw_ref[0, kk:kk + 1, c * LANES:(c + 1) * LANES])
                halves.append(acc)
            y = jnp.concatenate(halves, axis=1) + conv_b
            mu = jnp.mean(y, axis=-1, keepdims=True)
            yc = y - mu
            var = jnp.mean(yc * yc, axis=-1, keepdims=True)
            yn = (yc * lax.rsqrt(var + EPS)) * ln_g + ln_b
            pieces.append(_silu(yn).astype(BF16))
        z = jnp.concatenate(pieces, axis=0)
        a = jnp.dot(z, pw_ref[0], preferred_element_type=F32)
        mix_ref[rows, 0:CONV_W] = (a * gate_ref[rows, 0:CONV_W]).astype(BF16)
        return carry

    lax.fori_loop(0, n_chunks, conv_chunk, 0)

    n_blocks = L // BLOCK
    n_local = 3 * BLOCK if latent else L
    lo_q = _lane_iota((BLOCK, LANES)) < HEAD_DIM
    contract_last = (((1,), (1,)), ((), ()))

    def fold(pieces, op):
        acc = None
        for piece in pieces:
            for c in range(piece.shape[1] // LANES):
                slab = piece[:, c * LANES:(c + 1) * LANES]
                acc = slab if acc is None else op(acc, slab)
        return acc

    def attn_block(i, carry):
        q0 = pl.multiple_of(i * BLOCK, BLOCK)
        qrows = pl.ds(q0, BLOCK)
        if latent:
            jq = lax.broadcasted_iota(jnp.int32, (BLOCK, BLOCK), 0)
            jk = lax.broadcasted_iota(jnp.int32, (BLOCK, BLOCK), 1)
            prev_ok = jk >= jq + jnp.where(i > 0, 0, BLOCK)
            next_ok = jk <= jq - jnp.where(i < n_blocks - 1, 0, BLOCK)
            local_rows = pl.ds(q0, n_local)
        else:
            local_rows = pl.ds(0, n_local)
        for g in range(N_KV_HEADS):
            qm = jnp.concatenate(
                [qm_ref[qrows, (g * GQA_GROUP + j) * LANES:(g * GQA_GROUP + j + 1) * LANES]
                 for j in range(GQA_GROUP)], axis=0)
            s_loc = lax.dot_general(qm, kk_ref[g, local_rows, :], contract_last,
                                    preferred_element_type=F32)
            if latent:
                s_ctx = lax.dot_general(qm, ckk_ref[g], contract_last,
                                        preferred_element_type=F32)
            p_loc, p_ctx, inv = [], [], []
            for j in range(GQA_GROUP):
                hrows = slice(j * BLOCK, (j + 1) * BLOCK)
                sink = sink_ref[l, g * GQA_GROUP + j]
                sl = s_loc[hrows]
                if latent:
                    pieces = [jnp.where(prev_ok, sl[:, 0:BLOCK], NEG_INF),
                              sl[:, BLOCK:2 * BLOCK],
                              jnp.where(next_ok, sl[:, 2 * BLOCK:3 * BLOCK], NEG_INF),
                              s_ctx[hrows]]
                else:
                    pieces = [sl]
                m = jnp.maximum(jnp.max(fold(pieces, jnp.maximum), axis=-1, keepdims=True), sink)
                e = [jnp.exp(piece - m) for piece in pieces]
                denom = jnp.sum(fold(e, jnp.add), axis=-1, keepdims=True) + jnp.exp(sink - m)
                if latent:
                    p_loc.append(jnp.concatenate(e[0:3], axis=1).astype(BF16))
                    p_ctx.append(e[3].astype(BF16))
                else:
                    p_loc.append(e[0].astype(BF16))
                inv.append(1.0 / denom)
            o = jnp.dot(jnp.concatenate(p_loc, axis=0), vv_ref[g, local_rows, :],
                        preferred_element_type=F32)
            if latent:
                o = o + jnp.dot(jnp.concatenate(p_ctx, axis=0), cvv_ref[g],
                                preferred_element_type=F32)
            for p in range(GQA_GROUP // 2):
                o_lo = o[(2 * p) * BLOCK:(2 * p + 1) * BLOCK] * inv[2 * p]
                o_hi = o[(2 * p + 1) * BLOCK:(2 * p + 2) * BLOCK] * inv[2 * p + 1]
                c0 = CONV_W + (g * (GQA_GROUP // 2) + p) * LANES
                b_out = jnp.where(lo_q, o_lo, o_hi)
                mix_ref[qrows, c0:c0 + LANES] = (b_out * gate_ref[qrows, c0:c0 + LANES]).astype(BF16)
        return carry

    lax.fori_loop(0, n_blocks, attn_block, 0)

    def pool_chunk(i, carry):
        r0 = pl.multiple_of(i * ROW_CHUNK, ROW_CHUNK)
        rows = pl.ds(r0, ROW_CHUNK)

        t = r0 + lax.broadcasted_iota(jnp.int32, (ROW_CHUNK, LANES), 0)
        first_group = _lane_iota((ROW_CHUNK, LANES)) < POOL_GROUP_W
        deltas = []
        for c in range(POOL_W // LANES):

            def tok(off, c=c):
                return cpad_ref[c, pl.ds(r0 + POOL_PAD + off, ROW_CHUNK), :]

            centre = tok(0)
            half_a, half_b = POOL_WINDOWS[2 * c] // 2, POOL_WINDOWS[2 * c + 1] // 2
            sum_a = centre
            for off in list(range(-half_a, 0)) + list(range(1, half_a)):
                sum_a = sum_a + tok(off)
            sum_b = sum_a
            for off in list(range(-half_b, -half_a)) + list(range(half_a, half_b)):
                sum_b = sum_b + tok(off)
            half_w = jnp.where(first_group, half_a, half_b)
            cnt = jnp.minimum(t + half_w, L) - jnp.maximum(t - half_w, 0)
            mean = jnp.where(first_group, sum_a, sum_b) / cnt.astype(F32)
            deltas.append((mean - centre).astype(BF16))
        d = jnp.concatenate(deltas, axis=1)
        c_out = jnp.dot(d, pbd_ref[0], preferred_element_type=F32) * pool_scale
        c0 = CONV_W + ATTN_W
        mix_ref[rows, c0:D_MODEL] = (c_out * gate_ref[rows, c0:D_MODEL]).astype(BF16)
        return carry

    lax.fori_loop(0, n_chunks, pool_chunk, 0)

    def out_chunk(i, carry):
        r0 = pl.multiple_of(i * ROW_CHUNK, ROW_CHUNK)
        rows = pl.ds(r0, ROW_CHUNK)
        y = jnp.dot(mix_ref[rows, :], wout_ref[0], preferred_element_type=F32)
        xs_ref[rows, :] = xs_ref[rows, :] + gate * y
        return carry

    lax.fori_loop(0, n_chunks, out_chunk, 0)

    @pl.when(l == DEPTH - 1)
    def _():
        def final_chunk(i, carry):
            r0 = pl.multiple_of(i * ROW_CHUNK, ROW_CHUNK)
            rows = pl.ds(r0, ROW_CHUNK)
            o_ref[0, rows, :] = _rms(xs_ref[rows, :]) * fnw_ref[...]
            return carry

        lax.fori_loop(0, n_chunks, final_chunk, 0)


def _mixer_layers(x, mods, mod_row, weights, rope=None, cache=None):
    S, L, _ = x.shape
    latent = cache is not None
    sink, norm_w, w_in, conv_dw, vecs, conv_pw, pool_bd, w_out, final_norm_w = weights
    kpad = BLOCK if latent else 0

    per_layer = lambda *tail: (lambda s, l: (l,) + tail)
    in_specs = [
        pl.BlockSpec(memory_space=pltpu.SMEM),
        pl.BlockSpec((1, L, D_MODEL), lambda s, l: (s, 0, 0)),
        pl.BlockSpec((1, 1, 3, D_MODEL), lambda s, l: (l, mod_row(s), 0, 0)),
        pl.BlockSpec((1, 1, D_MODEL), per_layer(0, 0)),
        pl.BlockSpec((1, D_MODEL, IN_W), per_layer(0, 0)),
        pl.BlockSpec((1, 32, CONV_W), per_layer(0, 0)),
        pl.BlockSpec((1, 8, CONV_W), per_layer(0, 0)),
        pl.BlockSpec((1, CONV_W, CONV_W), per_layer(0, 0)),
        pl.BlockSpec((1, POOL_W, POOL_W), per_layer(0, 0)),
        pl.BlockSpec((1, D_MODEL, D_MODEL), per_layer(0, 0)),
        pl.BlockSpec((1, D_MODEL), lambda s, l: (0, 0)),
    ]
    args = [sink, x, mods, norm_w, w_in, conv_dw, vecs, conv_pw, pool_bd, w_out, final_norm_w]
    out_specs = [pl.BlockSpec((1, L, D_MODEL), lambda s, l: (s, 0, 0))]
    out_shape = [jax.ShapeDtypeStruct((S, L, D_MODEL), F32)]
    scratch = [
        pltpu.VMEM((L, D_MODEL), F32),
        pltpu.VMEM((CONV_W // LANES, L + 2 * CONV_PAD, LANES), F32),
        pltpu.VMEM((POOL_W // LANES, L + 2 * POOL_PAD, LANES), F32),
        pltpu.VMEM((L, D_MODEL), F32),
        pltpu.VMEM((L, 2 * ATTN_W), BF16),
        pltpu.VMEM((N_KV_HEADS, L + 2 * kpad, LANES), BF16),
        pltpu.VMEM((N_KV_HEADS, L + 2 * kpad, LANES), BF16),
        pltpu.VMEM((L, D_MODEL), BF16),
    ]
    if latent:
        cos, sin = rope
        ck, cv = cache
        past = ck.shape[2]
        in_specs += [
            pl.BlockSpec((L, LANES), lambda s, l: (0, 0)),
            pl.BlockSpec((L, LANES), lambda s, l: (0, 0)),
            pl.BlockSpec((1, 1, past, KV_W), lambda s, l: (s, l, 0, 0)),
            pl.BlockSpec((1, 1, past, KV_W), lambda s, l: (s, l, 0, 0)),
        ]
        args += [cos, sin, ck, cv]
        scratch += [pltpu.VMEM((N_KV_HEADS, past, LANES), BF16),
                    pltpu.VMEM((N_KV_HEADS, past, LANES), BF16)]
    else:
        out_specs += [pl.BlockSpec((1, 1, L, KV_W), lambda s, l: (s, l, 0, 0))] * 2
        out_shape += [jax.ShapeDtypeStruct((S, DEPTH, L, KV_W), F32)] * 2

    return pl.pallas_call(
        functools.partial(_layers_kernel, L=L, latent=latent),
        grid=(S, DEPTH),
        in_specs=in_specs,
        out_specs=out_specs,
        out_shape=out_shape,
        scratch_shapes=scratch,
        compiler_params=pltpu.CompilerParams(
            dimension_semantics=("arbitrary", "arbitrary"),
            vmem_limit_bytes=VMEM_LIMIT_BYTES),
        name="latent_layers" if latent else "context_layers",
    )(*args)


def _rope_tables(L):
    quarter = HEAD_DIM // 4
    t = jnp.arange(L)
    rows = (t // GRID_W).astype(F32)
    cols = (t % GRID_W).astype(F32)
    freqs = ROPE_BASE ** (-jnp.arange(quarter, dtype=F32) / quarter)
    ang_r = rows[:, None] * freqs[None, :]
    ang_c = cols[:, None] * freqs[None, :]
    cos = jnp.concatenate([jnp.cos(ang_r)] * 2 + [jnp.cos(ang_c)] * 2, axis=-1)
    sin = jnp.concatenate([-jnp.sin(ang_r), jnp.sin(ang_r), -jnp.sin(ang_c), jnp.sin(ang_c)], axis=-1)
    return jnp.tile(cos, (1, 2)), jnp.tile(sin, (1, 2))


def kernel(x_prompt, x_sample, c, cache_k, cache_v, c_ctx, w_ada, b_ada, norm_w, w_in, conv_dw, conv_b,
           conv_ln_g, conv_ln_b, conv_pw, attn_sink, pool_w, pool_scale, w_out, final_norm_w):
    n_ctx, ctx_len, _ = x_prompt.shape
    n_lat, lat_len, _ = x_sample.shape
    past = cache_k.shape[2]
    assert n_lat + 1 <= MOD_ROWS

    cvecs = jnp.zeros((MOD_ROWS, D_MODEL), F32).at[:n_lat].set(c).at[n_lat].set(c_ctx)
    mods = _adaln(cvecs, w_ada, b_ada).reshape(DEPTH, MOD_ROWS, 3, D_MODEL)

    n_groups = len(POOL_WINDOWS)
    eye = jnp.eye(n_groups, dtype=F32)
    pool_bd = (pool_w[:, :, :, None, :] * eye[None, :, None, :, None]).reshape(DEPTH, POOL_W, POOL_W)
    vecs = jnp.zeros((DEPTH, 8, CONV_W), F32)
    vecs = vecs.at[:, 0].set(conv_b).at[:, 1].set(conv_ln_g).at[:, 2].set(conv_ln_b).at[:, 3].set(pool_scale)
    weights = (
        attn_sink,
        norm_w.reshape(DEPTH, 1, D_MODEL),
        w_in.astype(BF16),
        jnp.pad(conv_dw, ((0, 0), (0, 32 - CONV_K), (0, 0))),
        vecs,
        conv_pw.astype(BF16),
        pool_bd.astype(BF16),
        w_out.astype(BF16),
        final_norm_w.reshape(1, D_MODEL),
    )

    y_prompt, new_k, new_v = _mixer_layers(x_prompt, mods, lambda s: n_lat, weights)
    new_cache_k = new_k.reshape(n_ctx, DEPTH, ctx_len, N_KV_HEADS, HEAD_DIM)
    new_cache_v = new_v.reshape(n_ctx, DEPTH, ctx_len, N_KV_HEADS, HEAD_DIM)

    (y_sample,) = _mixer_layers(
        x_sample, mods, lambda s: s, weights,
        rope=_rope_tables(lat_len),
        cache=(cache_k.reshape(n_lat, DEPTH, past, KV_W), cache_v.reshape(n_lat, DEPTH, past, KV_W)))
    return (y_prompt, y_sample, new_cache_k, new_cache_v)
```

```python
import functools

import jax
import jax.numpy as jnp
from jax import lax
from jax.experimental import pallas as pl
from jax.experimental.pallas import tpu as pltpu

D_MODEL = 1024
DEPTH = 4
GRID_W = 64
CONV_W = 256
ATTN_W = 512
POOL_W = 256
HEAD_DIM = 64
N_HEADS = 8
N_KV_HEADS = 2
GQA_GROUP = 4
KV_W = 128
WINDOW = 128
BLOCK = 128
CONV_K = 31
POOL_WINDOWS = (2, 4, 8, 16)
POOL_GROUP_W = 64
ROPE_BASE = 10000.0
EPS = 1e-6
NEG_INF = -1e30

OFF_A = 0
OFF_Q = 3 * CONV_W
OFF_KV = OFF_Q + ATTN_W
OFF_B_GATE = OFF_KV + 2 * KV_W
OFF_C = OFF_B_GATE + ATTN_W
IN_W = OFF_C + 2 * POOL_W

LANES = 128
CONV_PAD = 16
POOL_PAD = 8
ROW_CHUNK = 256
CONV_TILE = 32
MOD_ROWS = 16
VMEM_LIMIT_BYTES = 56 * 1024 * 1024

F32 = jnp.float32
BF16 = jnp.bfloat16


def _silu(x):
    return x * jax.nn.sigmoid(x)


def _lane_iota(shape):
    return lax.broadcasted_iota(jnp.int32, shape, 1)


def _dup_halves(x):
    r = pltpu.roll(x, HEAD_DIM, axis=1)
    lo = _lane_iota(x.shape) < HEAD_DIM
    return jnp.where(lo, x, r), jnp.where(lo, r, x)


def _rope(x, cos, sin):
    quarter = HEAD_DIM // 4
    fwd = pltpu.roll(x, LANES - quarter, axis=1)
    bwd = pltpu.roll(x, quarter, axis=1)
    first = (_lane_iota(x.shape) & (2 * quarter - 1)) < quarter
    return x * cos + jnp.where(first, fwd, bwd) * sin


def _rms(x):
    return x * lax.rsqrt(jnp.mean(x * x, axis=-1, keepdims=True) + EPS)


def _adaln_kernel(c_ref, w_ref, b_ref, o_ref):
    s = _silu(c_ref[...])
    o_ref[0] = jnp.dot(s, w_ref[0], precision=lax.Precision.HIGHEST,
                       preferred_element_type=F32) + b_ref[0]


def _adaln(cvecs, w_ada, b_ada):
    n_col = 3
    return pl.pallas_call(
        _adaln_kernel,
        grid=(DEPTH, n_col),
        in_specs=[
            pl.BlockSpec((MOD_ROWS, D_MODEL), lambda l, j: (0, 0)),
            pl.BlockSpec((1, D_MODEL, D_MODEL), lambda l, j: (l, 0, j)),
            pl.BlockSpec((1, 1, D_MODEL), lambda l, j: (l, 0, j)),
        ],
        out_specs=pl.BlockSpec((1, MOD_ROWS, D_MODEL), lambda l, j: (l, 0, j)),
        out_shape=jax.ShapeDtypeStruct((DEPTH, MOD_ROWS, 3 * D_MODEL), F32),
        compiler_params=pltpu.CompilerParams(
            dimension_semantics=("arbitrary", "arbitrary"),
            vmem_limit_bytes=VMEM_LIMIT_BYTES),
        name="adaln_mod",
    )(cvecs, w_ada, b_ada.reshape(DEPTH, 1, 3 * D_MODEL))


def _layers_kernel(*refs, L, latent):
    if latent:
        (sink_ref, x_ref, mod_ref, nw_ref, win_ref, dw_ref, vec_ref, pw_ref, pbd_ref, wout_ref,
         fnw_ref, cos_ref, sin_ref, ck_ref, cv_ref,
         o_ref,
         gpad_ref, cpad_ref, gate_ref, qm_ref, kk_ref, vv_ref, mix_ref, s_ref,
         ckk_ref, cvv_ref) = refs
    else:
        (sink_ref, x_ref, mod_ref, nw_ref, win_ref, dw_ref, vec_ref, pw_ref, pbd_ref, wout_ref,
         fnw_ref,
         o_ref, nk_ref, nv_ref,
         gpad_ref, cpad_ref, gate_ref, qm_ref, kk_ref, vv_ref, mix_ref, s_ref) = refs

    l = pl.program_id(1)
    n_chunks = L // ROW_CHUNK
    kpad = BLOCK if latent else 0

    @pl.when(l == 0)
    def _():
        o_ref[0] = x_ref[0]

    for c in range(CONV_W // LANES):
        gpad_ref[c, 0:CONV_PAD, :] = jnp.zeros((CONV_PAD, LANES), F32)
        gpad_ref[c, CONV_PAD + L:CONV_PAD + L + CONV_PAD, :] = jnp.zeros((CONV_PAD, LANES), F32)
        cpad_ref[c, 0:POOL_PAD, :] = jnp.zeros((POOL_PAD, LANES), F32)
        cpad_ref[c, POOL_PAD + L:POOL_PAD + L + POOL_PAD, :] = jnp.zeros((POOL_PAD, LANES), F32)
    if latent:
        for g in range(N_KV_HEADS):
            for ref in (kk_ref, vv_ref):
                ref[g, 0:kpad, :] = jnp.zeros((kpad, LANES), BF16)
                ref[g, kpad + L:kpad + L + kpad, :] = jnp.zeros((kpad, LANES), BF16)
        ck0, ck1 = _dup_halves(ck_ref[0, 0])
        cv0, cv1 = _dup_halves(cv_ref[0, 0])
        ckk_ref[0] = ck0.astype(BF16)
        ckk_ref[1] = ck1.astype(BF16)
        cvv_ref[0] = cv0.astype(BF16)
        cvv_ref[1] = cv1.astype(BF16)

    shift = mod_ref[0, 0, 0:1, :]
    scl = mod_ref[0, 0, 1:2, :]
    gate = mod_ref[0, 0, 2:3, :]

    def proj_chunk(i, carry):
        r0 = pl.multiple_of(i * ROW_CHUNK, ROW_CHUNK)
        rows = pl.ds(r0, ROW_CHUNK)
        x = o_ref[0, rows, :]
        h = (_rms(x) * nw_ref[0]) * (1.0 + scl) + shift
        hb = h.astype(BF16)

        ua = jnp.dot(hb, win_ref[0, :, OFF_A:OFF_Q], preferred_element_type=F32)
        glu = ua[:, 0:CONV_W] * jax.nn.sigmoid(ua[:, CONV_W:2 * CONV_W])
        for c in range(CONV_W // LANES):
            gpad_ref[c, pl.ds(r0 + CONV_PAD, ROW_CHUNK), :] = glu[:, c * LANES:(c + 1) * LANES]
        gate_ref[rows, 0:CONV_W] = _silu(ua[:, 2 * CONV_W:3 * CONV_W])

        uq = jnp.dot(hb, win_ref[0, :, OFF_Q:OFF_KV], preferred_element_type=F32)
        ukv = jnp.dot(hb, win_ref[0, :, OFF_KV:OFF_B_GATE], preferred_element_type=F32)
        k = ukv[:, 0:KV_W]
        v = ukv[:, KV_W:2 * KV_W]
        if latent:
            cos = cos_ref[rows, :]
            sin = sin_ref[rows, :]
            k = _rope(k, cos, sin)
        else:
            nk_ref[0, 0, rows, :] = k
            nv_ref[0, 0, rows, :] = v
        lo = _lane_iota((ROW_CHUNK, LANES)) < HEAD_DIM
        scale = HEAD_DIM ** -0.5
        for p in range(ATTN_W // LANES):
            qp = uq[:, p * LANES:(p + 1) * LANES]
            if latent:
                qp = _rope(qp, cos, sin)
            qp = qp * scale
            g, c = divmod(2 * p, GQA_GROUP)
            qm_ref[g, rows, c * LANES:(c + 1) * LANES] = jnp.where(lo, qp, 0.0).astype(BF16)
            qm_ref[g, rows, (c + 1) * LANES:(c + 2) * LANES] = jnp.where(lo, 0.0, qp).astype(BF16)
        k0, k1 = _dup_halves(k)
        v0, v1 = _dup_halves(v)
        krows = pl.ds(r0 + kpad, ROW_CHUNK)
        kk_ref[0, krows, :] = k0.astype(BF16)
        kk_ref[1, krows, :] = k1.astype(BF16)
        vv_ref[0, krows, :] = v0.astype(BF16)
        vv_ref[1, krows, :] = v1.astype(BF16)

        ub = jnp.dot(hb, win_ref[0, :, OFF_B_GATE:OFF_C], preferred_element_type=F32)
        gate_ref[rows, CONV_W:CONV_W + ATTN_W] = _silu(ub)

        uc = jnp.dot(hb, win_ref[0, :, OFF_C:IN_W], preferred_element_type=F32)
        for c in range(POOL_W // LANES):
            cpad_ref[c, pl.ds(r0 + POOL_PAD, ROW_CHUNK), :] = uc[:, c * LANES:(c + 1) * LANES]
        gate_ref[rows, CONV_W + ATTN_W:D_MODEL] = _silu(uc[:, POOL_W:2 * POOL_W])
        return carry

    lax.fori_loop(0, n_chunks, proj_chunk, 0)

    conv_b = vec_ref[0, 0:1, :]
    ln_g = vec_ref[0, 1:2, :]
    ln_b = vec_ref[0, 2:3, :]
    pool_scale = vec_ref[0, 3:4, :]

    def conv_chunk(i, carry):
        r0 = pl.multiple_of(i * ROW_CHUNK, ROW_CHUNK)
        rows = pl.ds(r0, ROW_CHUNK)
        pieces = []
        for t in range(ROW_CHUNK // CONV_TILE):
            base = r0 + t * CONV_TILE + (CONV_PAD - CONV_K // 2)
            halves = []
            for c in range(CONV_W // LANES):
                acc = jnp.zeros((CONV_TILE, LANES), F32)
                for kk in range(CONV_K):
                    acc = acc + (gpad_ref[c, pl.ds(base + kk, CONV_TILE), :]
                                 * dw_ref[0, kk:kk + 1, c * LANES:(c + 1) * LANES])
                halves.append(acc)
            y = jnp.concatenate(halves, axis=1) + conv_b
            mu = jnp.mean(y, axis=-1, keepdims=True)
            yc = y - mu
            var = jnp.mean(yc * yc, axis=-1, keepdims=True)
            yn = (yc * lax.rsqrt(var + EPS)) * ln_g + ln_b
            pieces.append(_silu(yn).astype(BF16))
        z = jnp.concatenate(pieces, axis=0)
        a = jnp.dot(z, pw_ref[0], preferred_element_type=F32)
        mix_ref[rows, 0:CONV_W] = (a * gate_ref[rows, 0:CONV_W]).astype(BF16)
        return carry

    lax.fori_loop(0, n_chunks, conv_chunk, 0)

    n_blocks = L // BLOCK
    n_local = 3 * BLOCK if latent else L
    lo_q = _lane_iota((BLOCK, LANES)) < HEAD_DIM
    contract_last = (((1,), (1,)), ((), ()))

    def fold(pieces, op):
        acc = None
        for piece in pieces:
            for c in range(piece.shape[1] // LANES):
                slab = piece[:, c * LANES:(c + 1) * LANES]
                acc = slab if acc is None else op(acc, slab)
        return acc

    def block_rows(i):
        q0 = i * BLOCK if isinstance(i, int) else pl.multiple_of(i * BLOCK, BLOCK)
        return q0, (pl.ds(q0, n_local) if latent else pl.ds(0, n_local))

    def scores(i, g):
        q0, local_rows = block_rows(i)
        qm = jnp.concatenate([qm_ref[g, pl.ds(q0, BLOCK), j * LANES:(j + 1) * LANES]
                              for j in range(GQA_GROUP)], axis=0)
        s_ref[g, :, 0:n_local] = lax.dot_general(qm, kk_ref[g, local_rows, :], contract_last,
                                                 preferred_element_type=F32)
        if latent:
            s_ref[g, :, n_local:] = lax.dot_general(qm, ckk_ref[g], contract_last,
                                                    preferred_element_type=F32)

    def softmax_pv(i, g):
        q0, local_rows = block_rows(i)
        qrows = pl.ds(q0, BLOCK)
        if latent:
            jq = lax.broadcasted_iota(jnp.int32, (BLOCK, BLOCK), 0)
            jk = lax.broadcasted_iota(jnp.int32, (BLOCK, BLOCK), 1)
            prev_ok = jk >= jq + jnp.where(i > 0, 0, BLOCK)
            next_ok = jk <= jq - jnp.where(i < n_blocks - 1, 0, BLOCK)
        p_loc, p_ctx, inv = [], [], []
        for j in range(GQA_GROUP):
            sink = sink_ref[l, g * GQA_GROUP + j]
            sj = s_ref[g, j * BLOCK:(j + 1) * BLOCK, :]
            if latent:
                pieces = [jnp.where(prev_ok, sj[:, 0:BLOCK], NEG_INF),
                          sj[:, BLOCK:2 * BLOCK],
                          jnp.where(next_ok, sj[:, 2 * BLOCK:3 * BLOCK], NEG_INF),
                          sj[:, n_local:]]
            else:
                pieces = [sj]
            m = jnp.maximum(jnp.max(fold(pieces, jnp.maximum), axis=-1, keepdims=True), sink)
            e = [jnp.exp(piece - m) for piece in pieces]
            denom = jnp.sum(fold(e, jnp.add), axis=-1, keepdims=True) + jnp.exp(sink - m)
            if latent:
                p_loc.append(jnp.concatenate(e[0:3], axis=1).astype(BF16))
                p_ctx.append(e[3].astype(BF16))
            else:
                p_loc.append(e[0].astype(BF16))
            inv.append(1.0 / denom)
        o = jnp.dot(jnp.concatenate(p_loc, axis=0), vv_ref[g, local_rows, :],
                    preferred_element_type=F32)
        if latent:
            o = o + jnp.dot(jnp.concatenate(p_ctx, axis=0), cvv_ref[g], preferred_element_type=F32)
        pairs = []
        for p in range(GQA_GROUP // 2):
            o_lo = o[(2 * p) * BLOCK:(2 * p + 1) * BLOCK] * inv[2 * p]
            o_hi = o[(2 * p + 1) * BLOCK:(2 * p + 2) * BLOCK] * inv[2 * p + 1]
            pairs.append(jnp.where(lo_q, o_lo, o_hi))
        c0 = CONV_W + g * (ATTN_W // N_KV_HEADS)
        cols = slice(c0, c0 + ATTN_W // N_KV_HEADS)
        mix_ref[qrows, cols] = (jnp.concatenate(pairs, axis=1) * gate_ref[qrows, cols]).astype(BF16)

    def attn_block(i, carry):
        scores(i, 1)
        softmax_pv(i, 0)
        scores(jnp.minimum(i + 1, n_blocks - 1), 0)
        softmax_pv(i, 1)
        return carry

    scores(0, 0)
    lax.fori_loop(0, n_blocks, attn_block, 0)

    def pool_chunk(i, carry):
        r0 = pl.multiple_of(i * ROW_CHUNK, ROW_CHUNK)
        rows = pl.ds(r0, ROW_CHUNK)
        t = r0 + lax.broadcasted_iota(jnp.int32, (ROW_CHUNK, LANES), 0)
        first_group = _lane_iota((ROW_CHUNK, LANES)) < POOL_GROUP_W
        deltas = []
        for c in range(POOL_W // LANES):

            def tok(off, c=c):
                return cpad_ref[c, pl.ds(r0 + POOL_PAD + off, ROW_CHUNK), :]

            centre = tok(0)
            half_a, half_b = POOL_WINDOWS[2 * c] // 2, POOL_WINDOWS[2 * c + 1] // 2
            sum_a = centre
            for off in list(range(-half_a, 0)) + list(range(1, half_a)):
                sum_a = sum_a + tok(off)
            sum_b = sum_a
            for off in list(range(-half_b, -half_a)) + list(range(half_a, half_b)):
                sum_b = sum_b + tok(off)
            half_w = jnp.where(first_group, half_a, half_b)
            cnt = jnp.minimum(t + half_w, L) - jnp.maximum(t - half_w, 0)
            mean = jnp.where(first_group, sum_a, sum_b) / cnt.astype(F32)
            deltas.append((mean - centre).astype(BF16))
        d = jnp.concatenate(deltas, axis=1)
        c_out = jnp.dot(d, pbd_ref[0], preferred_element_type=F32) * pool_scale
        c0 = CONV_W + ATTN_W
        mix_ref[rows, c0:D_MODEL] = (c_out * gate_ref[rows, c0:D_MODEL]).astype(BF16)
        return carry

    lax.fori_loop(0, n_chunks, pool_chunk, 0)

    def out_chunk(i, carry):
        r0 = pl.multiple_of(i * ROW_CHUNK, ROW_CHUNK)
        rows = pl.ds(r0, ROW_CHUNK)
        y = jnp.dot(mix_ref[rows, :], wout_ref[0], preferred_element_type=F32)
        o_ref[0, rows, :] = o_ref[0, rows, :] + gate * y
        return carry

    lax.fori_loop(0, n_chunks, out_chunk, 0)

    @pl.when(l == DEPTH - 1)
    def _():
        def final_chunk(i, carry):
            r0 = pl.multiple_of(i * ROW_CHUNK, ROW_CHUNK)
            rows = pl.ds(r0, ROW_CHUNK)
            o_ref[0, rows, :] = _rms(o_ref[0, rows, :]) * fnw_ref[...]
            return carry

        lax.fori_loop(0, n_chunks, final_chunk, 0)


def _mixer_layers(x, mods, mod_row, weights, rope=None, cache=None):
    S, L, _ = x.shape
    latent = cache is not None
    sink, norm_w, w_in, conv_dw, vecs, conv_pw, pool_bd, w_out, final_norm_w = weights
    kpad = BLOCK if latent else 0
    n_keys = 3 * BLOCK + cache[0].shape[2] if latent else L

    per_layer = lambda *tail: (lambda s, l: (l,) + tail)
    in_specs = [
        pl.BlockSpec(memory_space=pltpu.SMEM),
        pl.BlockSpec((1, L, D_MODEL), lambda s, l: (s, 0, 0)),
        pl.BlockSpec((1, 1, 3, D_MODEL), lambda s, l: (l, mod_row(s), 0, 0)),
        pl.BlockSpec((1, 1, D_MODEL), per_layer(0, 0)),
        pl.BlockSpec((1, D_MODEL, IN_W), per_layer(0, 0)),
        pl.BlockSpec((1, 32, CONV_W), per_layer(0, 0)),
        pl.BlockSpec((1, 8, CONV_W), per_layer(0, 0)),
        pl.BlockSpec((1, CONV_W, CONV_W), per_layer(0, 0)),
        pl.BlockSpec((1, POOL_W, POOL_W), per_layer(0, 0)),
        pl.BlockSpec((1, D_MODEL, D_MODEL), per_layer(0, 0)),
        pl.BlockSpec((1, D_MODEL), lambda s, l: (0, 0)),
    ]
    args = [sink, x, mods, norm_w, w_in, conv_dw, vecs, conv_pw, pool_bd, w_out, final_norm_w]
    out_specs = [pl.BlockSpec((1, L, D_MODEL), lambda s, l: (s, 0, 0))]
    out_shape = [jax.ShapeDtypeStruct((S, L, D_MODEL), F32)]
    scratch = [
        pltpu.VMEM((CONV_W // LANES, L + 2 * CONV_PAD, LANES), F32),
        pltpu.VMEM((POOL_W // LANES, L + 2 * POOL_PAD, LANES), F32),
        pltpu.VMEM((L, D_MODEL), F32),
        pltpu.VMEM((N_KV_HEADS, L, ATTN_W), BF16),
        pltpu.VMEM((N_KV_HEADS, L + 2 * kpad, LANES), BF16),
        pltpu.VMEM((N_KV_HEADS, L + 2 * kpad, LANES), BF16),
        pltpu.VMEM((L, D_MODEL), BF16),
        pltpu.VMEM((2, GQA_GROUP * BLOCK, n_keys), F32),
    ]
    if latent:
        cos, sin = rope
        ck, cv = cache
        past = ck.shape[2]
        in_specs += [
            pl.BlockSpec((L, LANES), lambda s, l: (0, 0)),
            pl.BlockSpec((L, LANES), lambda s, l: (0, 0)),
            pl.BlockSpec((1, 1, past, KV_W), lambda s, l: (s, l, 0, 0)),
            pl.BlockSpec((1, 1, past, KV_W), lambda s, l: (s, l, 0, 0)),
        ]
        args += [cos, sin, ck, cv]
        scratch += [pltpu.VMEM((N_KV_HEADS, past, LANES), BF16),
                    pltpu.VMEM((N_KV_HEADS, past, LANES), BF16)]
    else:
        out_specs += [pl.BlockSpec((1, 1, L, KV_W), lambda s, l: (s, l, 0, 0))] * 2
        out_shape += [jax.ShapeDtypeStruct((S, DEPTH, L, KV_W), F32)] * 2

    return pl.pallas_call(
        functools.partial(_layers_kernel, L=L, latent=latent),
        grid=(S, DEPTH),
        in_specs=in_specs,
        out_specs=out_specs,
        out_shape=out_shape,
        scratch_shapes=scratch,
        compiler_params=pltpu.CompilerParams(
            dimension_semantics=("arbitrary", "arbitrary"),
            vmem_limit_bytes=VMEM_LIMIT_BYTES),
        name="latent_layers" if latent else "context_layers",
    )(*args)


def _rope_tables(L):
    quarter = HEAD_DIM // 4
    t = jnp.arange(L)
    rows = (t // GRID_W).astype(F32)
    cols = (t % GRID_W).astype(F32)
    freqs = ROPE_BASE ** (-jnp.arange(quarter, dtype=F32) / quarter)
    ang_r = rows[:, None] * freqs[None, :]
    ang_c = cols[:, None] * freqs[None, :]
    cos = jnp.concatenate([jnp.cos(ang_r)] * 2 + [jnp.cos(ang_c)] * 2, axis=-1)
    sin = jnp.concatenate([-jnp.sin(ang_r), jnp.sin(ang_r), -jnp.sin(ang_c), jnp.sin(ang_c)], axis=-1)
    return jnp.tile(cos, (1, 2)), jnp.tile(sin, (1, 2))


def kernel(x_prompt, x_sample, c, cache_k, cache_v, c_ctx, w_ada, b_ada, norm_w, w_in, conv_dw, conv_b,
           conv_ln_g, conv_ln_b, conv_pw, attn_sink, pool_w, pool_scale, w_out, final_norm_w):
    n_ctx, ctx_len, _ = x_prompt.shape
    n_lat, lat_len, _ = x_sample.shape
    past = cache_k.shape[2]
    assert n_lat + 1 <= MOD_ROWS

    cvecs = jnp.zeros((MOD_ROWS, D_MODEL), F32).at[:n_lat].set(c).at[n_lat].set(c_ctx)
    mods = _adaln(cvecs, w_ada, b_ada).reshape(DEPTH, MOD_ROWS, 3, D_MODEL)

    n_groups = len(POOL_WINDOWS)
    eye = jnp.eye(n_groups, dtype=F32)
    pool_bd = (pool_w[:, :, :, None, :] * eye[None, :, None, :, None]).reshape(DEPTH, POOL_W, POOL_W)
    vecs = jnp.zeros((DEPTH, 8, CONV_W), F32)
    vecs = vecs.at[:, 0].set(conv_b).at[:, 1].set(conv_ln_g).at[:, 2].set(conv_ln_b).at[:, 3].set(pool_scale)
    weights = (
        attn_sink,
        norm_w.reshape(DEPTH, 1, D_MODEL),
        w_in.astype(BF16),
        jnp.pad(conv_dw, ((0, 0), (0, 32 - CONV_K), (0, 0))),
        vecs,
        conv_pw.astype(BF16),
        pool_bd.astype(BF16),
        w_out.astype(BF16),
        final_norm_w.reshape(1, D_MODEL),
    )

    y_prompt, new_k, new_v = _mixer_layers(x_prompt, mods, lambda s: n_lat, weights)
    new_cache_k = new_k.reshape(n_ctx, DEPTH, ctx_len, N_KV_HEADS, HEAD_DIM)
    new_cache_v = new_v.reshape(n_ctx, DEPTH, ctx_len, N_KV_HEADS, HEAD_DIM)

    (y_sample,) = _mixer_layers(
        x_sample, mods, lambda s: s, weights,
        rope=_rope_tables(lat_len),
        cache=(cache_k.reshape(n_lat, DEPTH, past, KV_W), cache_v.reshape(n_lat, DEPTH, past, KV_W)))
    return (y_prompt, y_sample, new_cache_k, new_cache_v)
```

```python
import functools

import jax
import jax.numpy as jnp
from jax import lax
from jax.experimental import pallas as pl
from jax.experimental.pallas import tpu as pltpu

D_MODEL = 1024
DEPTH = 4
GRID_W = 64
CONV_W = 256
ATTN_W = 512
POOL_W = 256
HEAD_DIM = 64
N_HEADS = 8
N_KV_HEADS = 2
GQA_GROUP = 4
KV_W = 128
WINDOW = 128
BLOCK = 128
CONV_K = 31
POOL_WINDOWS = (2, 4, 8, 16)
POOL_GROUP_W = 64
ROPE_BASE = 10000.0
EPS = 1e-6
NEG_INF = -1e30

OFF_A = 0
OFF_Q = 3 * CONV_W
OFF_KV = OFF_Q + ATTN_W
OFF_B_GATE = OFF_KV + 2 * KV_W
OFF_C = OFF_B_GATE + ATTN_W
IN_W = OFF_C + 2 * POOL_W

LANES = 128
CONV_PAD = 16
POOL_PAD = 8
ROW_CHUNK = 256
CONV_TILE = 32
PROJ_UNROLL = 2
ATTN_UNROLL = 2
OUT_UNROLL = 4
MOD_ROWS = 16
VMEM_LIMIT_BYTES = 56 * 1024 * 1024

F32 = jnp.float32
BF16 = jnp.bfloat16


def _silu(x):
    return x * jax.nn.sigmoid(x)


def _lane_iota(shape):
    return lax.broadcasted_iota(jnp.int32, shape, 1)


def _dup_halves(x):
    r = pltpu.roll(x, HEAD_DIM, axis=1)
    lo = _lane_iota(x.shape) < HEAD_DIM
    return jnp.where(lo, x, r), jnp.where(lo, r, x)


def _rope(x, cos, sin):
    quarter = HEAD_DIM // 4
    fwd = pltpu.roll(x, LANES - quarter, axis=1)
    bwd = pltpu.roll(x, quarter, axis=1)
    first = (_lane_iota(x.shape) & (2 * quarter - 1)) < quarter
    return x * cos + jnp.where(first, fwd, bwd) * sin


def _rms(x):
    return x * lax.rsqrt(jnp.mean(x * x, axis=-1, keepdims=True) + EPS)


def _adaln_kernel(c_ref, w_ref, b_ref, o_ref):
    s = _silu(c_ref[...])
    o_ref[0] = jnp.dot(s, w_ref[0], precision=lax.Precision.HIGHEST,
                       preferred_element_type=F32) + b_ref[0]


def _adaln(cvecs, w_ada, b_ada):
    n_col = 3
    return pl.pallas_call(
        _adaln_kernel,
        grid=(DEPTH, n_col),
        in_specs=[
            pl.BlockSpec((MOD_ROWS, D_MODEL), lambda l, j: (0, 0)),
            pl.BlockSpec((1, D_MODEL, D_MODEL), lambda l, j: (l, 0, j)),
            pl.BlockSpec((1, 1, D_MODEL), lambda l, j: (l, 0, j)),
        ],
        out_specs=pl.BlockSpec((1, MOD_ROWS, D_MODEL), lambda l, j: (l, 0, j)),
        out_shape=jax.ShapeDtypeStruct((DEPTH, MOD_ROWS, 3 * D_MODEL), F32),
        compiler_params=pltpu.CompilerParams(
            dimension_semantics=("arbitrary", "arbitrary"),
            vmem_limit_bytes=VMEM_LIMIT_BYTES),
        name="adaln_mod",
    )(cvecs, w_ada, b_ada.reshape(DEPTH, 1, 3 * D_MODEL))


def _layers_kernel(*refs, L, latent):
    if latent:
        (sink_ref, x_ref, mod_ref, nw_ref, win_ref, dw_ref, vec_ref, pw_ref, pbd_ref, wout_ref,
         fnw_ref, cos_ref, sin_ref, ck_ref, cv_ref,
         o_ref,
         gpad_ref, cpad_ref, gate_ref, qm_ref, kk_ref, vv_ref, mix_ref, s_ref,
         ckk_ref, cvv_ref) = refs
    else:
        (sink_ref, x_ref, mod_ref, nw_ref, win_ref, dw_ref, vec_ref, pw_ref, pbd_ref, wout_ref,
         fnw_ref,
         o_ref, nk_ref, nv_ref,
         gpad_ref, cpad_ref, gate_ref, qm_ref, kk_ref, vv_ref, mix_ref, s_ref) = refs

    l = pl.program_id(1)
    n_chunks = L // ROW_CHUNK
    kpad = BLOCK if latent else 0

    @pl.when(l == 0)
    def _():
        o_ref[0] = x_ref[0]

    for c in range(CONV_W // LANES):
        gpad_ref[c, 0:CONV_PAD, :] = jnp.zeros((CONV_PAD, LANES), F32)
        gpad_ref[c, CONV_PAD + L:CONV_PAD + L + CONV_PAD, :] = jnp.zeros((CONV_PAD, LANES), F32)
        cpad_ref[c, 0:POOL_PAD, :] = jnp.zeros((POOL_PAD, LANES), F32)
        cpad_ref[c, POOL_PAD + L:POOL_PAD + L + POOL_PAD, :] = jnp.zeros((POOL_PAD, LANES), F32)
    if latent:
        for g in range(N_KV_HEADS):
            for ref in (kk_ref, vv_ref):
                ref[g, 0:kpad, :] = jnp.zeros((kpad, LANES), BF16)
                ref[g, kpad + L:kpad + L + kpad, :] = jnp.zeros((kpad, LANES), BF16)
        ck0, ck1 = _dup_halves(ck_ref[0, 0])
        cv0, cv1 = _dup_halves(cv_ref[0, 0])
        ckk_ref[0] = ck0.astype(BF16)
        ckk_ref[1] = ck1.astype(BF16)
        cvv_ref[0] = cv0.astype(BF16)
        cvv_ref[1] = cv1.astype(BF16)

    shift = mod_ref[0, 0, 0:1, :]
    scl = mod_ref[0, 0, 1:2, :]
    gate = mod_ref[0, 0, 2:3, :]

    def proj_chunk(i, carry):
        r0 = pl.multiple_of(i * ROW_CHUNK, ROW_CHUNK)
        rows = pl.ds(r0, ROW_CHUNK)
        x = o_ref[0, rows, :]
        h = (_rms(x) * nw_ref[0]) * (1.0 + scl) + shift
        hb = h.astype(BF16)

        ua = jnp.dot(hb, win_ref[0, :, OFF_A:OFF_Q], preferred_element_type=F32)
        glu = ua[:, 0:CONV_W] * jax.nn.sigmoid(ua[:, CONV_W:2 * CONV_W])
        for c in range(CONV_W // LANES):
            gpad_ref[c, pl.ds(r0 + CONV_PAD, ROW_CHUNK), :] = glu[:, c * LANES:(c + 1) * LANES]
        gate_ref[rows, 0:CONV_W] = _silu(ua[:, 2 * CONV_W:3 * CONV_W])

        uq = jnp.dot(hb, win_ref[0, :, OFF_Q:OFF_KV], preferred_element_type=F32)
        ukv = jnp.dot(hb, win_ref[0, :, OFF_KV:OFF_B_GATE], preferred_element_type=F32)
        k = ukv[:, 0:KV_W]
        v = ukv[:, KV_W:2 * KV_W]
        if latent:
            cos = cos_ref[rows, :]
            sin = sin_ref[rows, :]
            k = _rope(k, cos, sin)
        else:
            nk_ref[0, 0, rows, :] = k
            nv_ref[0, 0, rows, :] = v
        lo = _lane_iota((ROW_CHUNK, LANES)) < HEAD_DIM
        scale = HEAD_DIM ** -0.5
        for p in range(ATTN_W // LANES):
            qp = uq[:, p * LANES:(p + 1) * LANES]
            if latent:
                qp = _rope(qp, cos, sin)
            qp = qp * scale
            g, c = divmod(2 * p, GQA_GROUP)
            qm_ref[g, rows, c * LANES:(c + 1) * LANES] = jnp.where(lo, qp, 0.0).astype(BF16)
            qm_ref[g, rows, (c + 1) * LANES:(c + 2) * LANES] = jnp.where(lo, 0.0, qp).astype(BF16)
        k0, k1 = _dup_halves(k)
        v0, v1 = _dup_halves(v)
        krows = pl.ds(r0 + kpad, ROW_CHUNK)
        kk_ref[0, krows, :] = k0.astype(BF16)
        kk_ref[1, krows, :] = k1.astype(BF16)
        vv_ref[0, krows, :] = v0.astype(BF16)
        vv_ref[1, krows, :] = v1.astype(BF16)

        ub = jnp.dot(hb, win_ref[0, :, OFF_B_GATE:OFF_C], preferred_element_type=F32)
        gate_ref[rows, CONV_W:CONV_W + ATTN_W] = _silu(ub)

        uc = jnp.dot(hb, win_ref[0, :, OFF_C:IN_W], preferred_element_type=F32)
        for c in range(POOL_W // LANES):
            cpad_ref[c, pl.ds(r0 + POOL_PAD, ROW_CHUNK), :] = uc[:, c * LANES:(c + 1) * LANES]
        gate_ref[rows, CONV_W + ATTN_W:D_MODEL] = _silu(uc[:, POOL_W:2 * POOL_W])
        return carry

    lax.fori_loop(0, n_chunks, proj_chunk, 0, unroll=min(n_chunks, PROJ_UNROLL))

    conv_b = vec_ref[0, 0:1, :]
    ln_g = vec_ref[0, 1:2, :]
    ln_b = vec_ref[0, 2:3, :]
    pool_scale = vec_ref[0, 3:4, :]

    def conv_rows(r0):
        rows = pl.ds(r0, BLOCK)
        pieces = []
        for t in range(BLOCK // CONV_TILE):
            base = r0 + t * CONV_TILE + (CONV_PAD - CONV_K // 2)
            halves = []
            for c in range(CONV_W // LANES):
                acc = jnp.zeros((CONV_TILE, LANES), F32)
                for kk in range(CONV_K):
                    acc = acc + (gpad_ref[c, pl.ds(base + kk, CONV_TILE), :]
                                 * dw_ref[0, kk:kk + 1, c * LANES:(c + 1) * LANES])
                halves.append(acc)
            y = jnp.concatenate(halves, axis=1) + conv_b
            mu = jnp.mean(y, axis=-1, keepdims=True)
            yc = y - mu
            var = jnp.mean(yc * yc, axis=-1, keepdims=True)
            yn = (yc * lax.rsqrt(var + EPS)) * ln_g + ln_b
            pieces.append(_silu(yn).astype(BF16))
        z = jnp.concatenate(pieces, axis=0)
        a = jnp.dot(z, pw_ref[0], preferred_element_type=F32)
        mix_ref[rows, 0:CONV_W] = (a * gate_ref[rows, 0:CONV_W]).astype(BF16)

    def pool_rows(r0):
        rows = pl.ds(r0, BLOCK)
        t = r0 + lax.broadcasted_iota(jnp.int32, (BLOCK, LANES), 0)
        first_group = _lane_iota((BLOCK, LANES)) < POOL_GROUP_W
        deltas = []
        for c in range(POOL_W // LANES):

            def tok(off, c=c):
                return cpad_ref[c, pl.ds(r0 + POOL_PAD + off, BLOCK), :]

            centre = tok(0)
            half_a, half_b = POOL_WINDOWS[2 * c] // 2, POOL_WINDOWS[2 * c + 1] // 2
            sum_a = centre
            for off in list(range(-half_a, 0)) + list(range(1, half_a)):
                sum_a = sum_a + tok(off)
            sum_b = sum_a
            for off in list(range(-half_b, -half_a)) + list(range(half_a, half_b)):
                sum_b = sum_b + tok(off)
            half_w = jnp.where(first_group, half_a, half_b)
            cnt = jnp.minimum(t + half_w, L) - jnp.maximum(t - half_w, 0)
            mean = jnp.where(first_group, sum_a, sum_b) / cnt.astype(F32)
            deltas.append((mean - centre).astype(BF16))
        d = jnp.concatenate(deltas, axis=1)
        c_out = jnp.dot(d, pbd_ref[0], preferred_element_type=F32) * pool_scale
        c0 = CONV_W + ATTN_W
        mix_ref[rows, c0:D_MODEL] = (c_out * gate_ref[rows, c0:D_MODEL]).astype(BF16)

    n_blocks = L // BLOCK
    n_local = 3 * BLOCK if latent else L
    lo_q = _lane_iota((BLOCK, LANES)) < HEAD_DIM
    contract_last = (((1,), (1,)), ((), ()))

    def fold(pieces, op):
        acc = None
        for piece in pieces:
            for c in range(piece.shape[1] // LANES):
                slab = piece[:, c * LANES:(c + 1) * LANES]
                acc = slab if acc is None else op(acc, slab)
        return acc

    def block_rows(i):
        q0 = i * BLOCK if isinstance(i, int) else pl.multiple_of(i * BLOCK, BLOCK)
        return q0, (pl.ds(q0, n_local) if latent else pl.ds(0, n_local))

    def scores(i, g):
        q0, local_rows = block_rows(i)
        qm = jnp.concatenate([qm_ref[g, pl.ds(q0, BLOCK), j * LANES:(j + 1) * LANES]
                              for j in range(GQA_GROUP)], axis=0)
        s_ref[g, :, 0:n_local] = lax.dot_general(qm, kk_ref[g, local_rows, :], contract_last,
                                                 preferred_element_type=F32)
        if latent:
            s_ref[g, :, n_local:] = lax.dot_general(qm, ckk_ref[g], contract_last,
                                                    preferred_element_type=F32)

    def softmax_pv(i, g):
        q0, local_rows = block_rows(i)
        qrows = pl.ds(q0, BLOCK)
        if latent:
            jq = lax.broadcasted_iota(jnp.int32, (BLOCK, BLOCK), 0)
            jk = lax.broadcasted_iota(jnp.int32, (BLOCK, BLOCK), 1)
            prev_ok = jk >= jq + jnp.where(i > 0, 0, BLOCK)
            next_ok = jk <= jq - jnp.where(i < n_blocks - 1, 0, BLOCK)
        p_loc, p_ctx, inv = [], [], []
        for j in range(GQA_GROUP):
            sink = sink_ref[l, g * GQA_GROUP + j]
            sj = s_ref[g, j * BLOCK:(j + 1) * BLOCK, :]
            if latent:
                pieces = [jnp.where(prev_ok, sj[:, 0:BLOCK], NEG_INF),
                          sj[:, BLOCK:2 * BLOCK],
                          jnp.where(next_ok, sj[:, 2 * BLOCK:3 * BLOCK], NEG_INF),
                          sj[:, n_local:]]
            else:
                pieces = [sj]
            m = jnp.maximum(jnp.max(fold(pieces, jnp.maximum), axis=-1, keepdims=True), sink)
            e = [jnp.exp(piece - m) for piece in pieces]
            denom = jnp.sum(fold(e, jnp.add), axis=-1, keepdims=True) + jnp.exp(sink - m)
            if latent:
                p_loc.append(jnp.concatenate(e[0:3], axis=1).astype(BF16))
                p_ctx.append(e[3].astype(BF16))
            else:
                p_loc.append(e[0].astype(BF16))
            inv.append(1.0 / denom)
        o = jnp.dot(jnp.concatenate(p_loc, axis=0), vv_ref[g, local_rows, :],
                    preferred_element_type=F32)
        if latent:
            o = o + jnp.dot(jnp.concatenate(p_ctx, axis=0), cvv_ref[g], preferred_element_type=F32)
        pairs = []
        for p in range(GQA_GROUP // 2):
            o_lo = o[(2 * p) * BLOCK:(2 * p + 1) * BLOCK] * inv[2 * p]
            o_hi = o[(2 * p + 1) * BLOCK:(2 * p + 2) * BLOCK] * inv[2 * p + 1]
            pairs.append(jnp.where(lo_q, o_lo, o_hi))
        c0 = CONV_W + g * (ATTN_W // N_KV_HEADS)
        cols = slice(c0, c0 + ATTN_W // N_KV_HEADS)
        mix_ref[qrows, cols] = (jnp.concatenate(pairs, axis=1) * gate_ref[qrows, cols]).astype(BF16)

    def attn_block(i, carry):
        scores(i, 1)
        softmax_pv(i, 0)
        scores(jnp.minimum(i + 1, n_blocks - 1), 0)
        softmax_pv(i, 1)
        r0 = pl.multiple_of(i * BLOCK, BLOCK)
        conv_rows(r0)
        pool_rows(r0)
        return carry

    scores(0, 0)
    lax.fori_loop(0, n_blocks, attn_block, 0, unroll=min(n_blocks, ATTN_UNROLL))

    def out_chunk(i, carry):
        r0 = pl.multiple_of(i * ROW_CHUNK, ROW_CHUNK)
        rows = pl.ds(r0, ROW_CHUNK)
        y = jnp.dot(mix_ref[rows, :], wout_ref[0], preferred_element_type=F32)
        o_ref[0, rows, :] = o_ref[0, rows, :] + gate * y
        return carry

    lax.fori_loop(0, n_chunks, out_chunk, 0, unroll=min(n_chunks, OUT_UNROLL))

    @pl.when(l == DEPTH - 1)
    def _():
        def final_chunk(i, carry):
            r0 = pl.multiple_of(i * ROW_CHUNK, ROW_CHUNK)
            rows = pl.ds(r0, ROW_CHUNK)
            o_ref[0, rows, :] = _rms(o_ref[0, rows, :]) * fnw_ref[...]
            return carry

        lax.fori_loop(0, n_chunks, final_chunk, 0)


def _mixer_layers(x, mods, mod_row, weights, rope=None, cache=None):
    S, L, _ = x.shape
    latent = cache is not None
    sink, norm_w, w_in, conv_dw, vecs, conv_pw, pool_bd, w_out, final_norm_w = weights
    kpad = BLOCK if latent else 0
    n_keys = 3 * BLOCK + cache[0].shape[2] if latent else L

    per_layer = lambda *tail: (lambda s, l: (l,) + tail)
    in_specs = [
        pl.BlockSpec(memory_space=pltpu.SMEM),
        pl.BlockSpec((1, L, D_MODEL), lambda s, l: (s, 0, 0)),
        pl.BlockSpec((1, 1, 3, D_MODEL), lambda s, l: (l, mod_row(s), 0, 0)),
        pl.BlockSpec((1, 1, D_MODEL), per_layer(0, 0)),
        pl.BlockSpec((1, D_MODEL, IN_W), per_layer(0, 0)),
        pl.BlockSpec((1, 32, CONV_W), per_layer(0, 0)),
        pl.BlockSpec((1, 8, CONV_W), per_layer(0, 0)),
        pl.BlockSpec((1, CONV_W, CONV_W), per_layer(0, 0)),
        pl.BlockSpec((1, POOL_W, POOL_W), per_layer(0, 0)),
        pl.BlockSpec((1, D_MODEL, D_MODEL), per_layer(0, 0)),
        pl.BlockSpec((1, D_MODEL), lambda s, l: (0, 0)),
    ]
    args = [sink, x, mods, norm_w, w_in, conv_dw, vecs, conv_pw, pool_bd, w_out, final_norm_w]
    out_specs = [pl.BlockSpec((1, L, D_MODEL), lambda s, l: (s, 0, 0))]
    out_shape = [jax.ShapeDtypeStruct((S, L, D_MODEL), F32)]
    scratch = [
        pltpu.VMEM((CONV_W // LANES, L + 2 * CONV_PAD, LANES), F32),
        pltpu.VMEM((POOL_W // LANES, L + 2 * POOL_PAD, LANES), F32),
        pltpu.VMEM((L, D_MODEL), F32),
        pltpu.VMEM((N_KV_HEADS, L, ATTN_W), BF16),
        pltpu.VMEM((N_KV_HEADS, L + 2 * kpad, LANES), BF16),
        pltpu.VMEM((N_KV_HEADS, L + 2 * kpad, LANES), BF16),
        pltpu.VMEM((L, D_MODEL), BF16),
        pltpu.VMEM((2, GQA_GROUP * BLOCK, n_keys), F32),
    ]
    if latent:
        cos, sin = rope
        ck, cv = cache
        past = ck.shape[2]
        in_specs += [
            pl.BlockSpec((L, LANES), lambda s, l: (0, 0)),
            pl.BlockSpec((L, LANES), lambda s, l: (0, 0)),
            pl.BlockSpec((1, 1, past, KV_W), lambda s, l: (s, l, 0, 0)),
            pl.BlockSpec((1, 1, past, KV_W), lambda s, l: (s, l, 0, 0)),
        ]
        args += [cos, sin, ck, cv]
        scratch += [pltpu.VMEM((N_KV_HEADS, past, LANES), BF16),
                    pltpu.VMEM((N_KV_HEADS, past, LANES), BF16)]
    else:
        out_specs += [pl.BlockSpec((1, 1, L, KV_W), lambda s, l: (s, l, 0, 0))] * 2
        out_shape += [jax.ShapeDtypeStruct((S, DEPTH, L, KV_W), F32)] * 2

    return pl.pallas_call(
        functools.partial(_layers_kernel, L=L, latent=latent),
        grid=(S, DEPTH),
        in_specs=in_specs,
        out_specs=out_specs,
        out_shape=out_shape,
        scratch_shapes=scratch,
        compiler_params=pltpu.CompilerParams(
            dimension_semantics=("arbitrary", "arbitrary"),
            vmem_limit_bytes=VMEM_LIMIT_BYTES),
        name="latent_layers" if latent else "context_layers",
    )(*args)


def _rope_tables(L):
    quarter = HEAD_DIM // 4
    t = jnp.arange(L)
    rows = (t // GRID_W).astype(F32)
    cols = (t % GRID_W).astype(F32)
    freqs = ROPE_BASE ** (-jnp.arange(quarter, dtype=F32) / quarter)
    ang_r = rows[:, None] * freqs[None, :]
    ang_c = cols[:, None] * freqs[None, :]
    cos = jnp.concatenate([jnp.cos(ang_r)] * 2 + [jnp.cos(ang_c)] * 2, axis=-1)
    sin = jnp.concatenate([-jnp.sin(ang_r), jnp.sin(ang_r), -jnp.sin(ang_c), jnp.sin(ang_c)], axis=-1)
    return jnp.tile(cos, (1, 2)), jnp.tile(sin, (1, 2))


def kernel(x_prompt, x_sample, c, cache_k, cache_v, c_ctx, w_ada, b_ada, norm_w, w_in, conv_dw, conv_b,
           conv_ln_g, conv_ln_b, conv_pw, attn_sink, pool_w, pool_scale, w_out, final_norm_w):
    n_ctx, ctx_len, _ = x_prompt.shape
    n_lat, lat_len, _ = x_sample.shape
    past = cache_k.shape[2]
    assert n_lat + 1 <= MOD_ROWS

    cvecs = jnp.zeros((MOD_ROWS, D_MODEL), F32).at[:n_lat].set(c).at[n_lat].set(c_ctx)
    mods = _adaln(cvecs, w_ada, b_ada).reshape(DEPTH, MOD_ROWS, 3, D_MODEL)

    n_groups = len(POOL_WINDOWS)
    eye = jnp.eye(n_groups, dtype=F32)
    pool_bd = (pool_w[:, :, :, None, :] * eye[None, :, None, :, None]).reshape(DEPTH, POOL_W, POOL_W)
    vecs = jnp.zeros((DEPTH, 8, CONV_W), F32)
    vecs = vecs.at[:, 0].set(conv_b).at[:, 1].set(conv_ln_g).at[:, 2].set(conv_ln_b).at[:, 3].set(pool_scale)
    weights = (
        attn_sink,
        norm_w.reshape(DEPTH, 1, D_MODEL),
        w_in.astype(BF16),
        jnp.pad(conv_dw, ((0, 0), (0, 32 - CONV_K), (0, 0))),
        vecs,
        conv_pw.astype(BF16),
        pool_bd.astype(BF16),
        w_out.astype(BF16),
        final_norm_w.reshape(1, D_MODEL),
    )

    y_prompt, new_k, new_v = _mixer_layers(x_prompt, mods, lambda s: n_lat, weights)
    new_cache_k = new_k.reshape(n_ctx, DEPTH, ctx_len, N_KV_HEADS, HEAD_DIM)
    new_cache_v = new_v.reshape(n_ctx, DEPTH, ctx_len, N_KV_HEADS, HEAD_DIM)

    (y_sample,) = _mixer_layers(
        x_sample, mods, lambda s: s, weights,
        rope=_rope_tables(lat_len),
        cache=(cache_k.reshape(n_lat, DEPTH, past, KV_W), cache_v.reshape(n_lat, DEPTH, past, KV_W)))
    return (y_prompt, y_sample, new_cache_k, new_cache_v)
```

```python
import functools

import jax
import jax.numpy as jnp
from jax import lax
from jax.experimental import pallas as pl
from jax.experimental.pallas import tpu as pltpu

D_MODEL = 1024
DEPTH = 4
GRID_W = 64
CONV_W = 256
ATTN_W = 512
POOL_W = 256
HEAD_DIM = 64
N_HEADS = 8
N_KV_HEADS = 2
GQA_GROUP = 4
KV_W = 128
WINDOW = 128
BLOCK = 128
CONV_K = 31
POOL_WINDOWS = (2, 4, 8, 16)
POOL_GROUP_W = 64
ROPE_BASE = 10000.0
EPS = 1e-6
NEG_INF = -1e30
LOG2E = 1.4426950408889634

OFF_A = 0
OFF_Q = 3 * CONV_W
OFF_KV = OFF_Q + ATTN_W
OFF_B_GATE = OFF_KV + 2 * KV_W
OFF_C = OFF_B_GATE + ATTN_W
IN_W = OFF_C + 2 * POOL_W

LANES = 128
CONV_PAD = 16
POOL_PAD = 8
ROW_CHUNK = 256
CONV_TILE = 32
ATTN_UNROLL = 2
ROW_SUMS_ON_MXU = {True: True, False: False}
SOFTMAX_ROWS = 32
MOD_ROWS = 16
VMEM_LIMIT_BYTES = 56 * 1024 * 1024

F32 = jnp.float32
BF16 = jnp.bfloat16


def _silu(x):
    return x * jax.nn.sigmoid(x)


def _lane_iota(shape):
    return lax.broadcasted_iota(jnp.int32, shape, 1)


def _dup_halves(x):
    r = pltpu.roll(x, HEAD_DIM, axis=1)
    lo = _lane_iota(x.shape) < HEAD_DIM
    return jnp.where(lo, x, r), jnp.where(lo, r, x)


def _with_ones(x):
    r = pltpu.roll(x, HEAD_DIM, axis=1)
    lo = _lane_iota(x.shape) < HEAD_DIM
    return jnp.where(lo, x, 1.0), jnp.where(lo, r, 1.0)


def _rope(x, cos, sin):
    quarter = HEAD_DIM // 4
    fwd = pltpu.roll(x, LANES - quarter, axis=1)
    bwd = pltpu.roll(x, quarter, axis=1)
    first = (_lane_iota(x.shape) & (2 * quarter - 1)) < quarter
    return x * cos + jnp.where(first, fwd, bwd) * sin


def _rms(x):
    return x * lax.rsqrt(jnp.mean(x * x, axis=-1, keepdims=True) + EPS)


def _adaln_kernel(c_ref, w_ref, b_ref, o_ref):
    s = _silu(c_ref[...])
    o_ref[0] = jnp.dot(s.astype(BF16), w_ref[0].astype(BF16), preferred_element_type=F32) + b_ref[0]


def _adaln(cvecs, w_ada, b_ada):
    n_col = 3
    return pl.pallas_call(
        _adaln_kernel,
        grid=(DEPTH, n_col),
        in_specs=[
            pl.BlockSpec((MOD_ROWS, D_MODEL), lambda l, j: (0, 0)),
            pl.BlockSpec((1, D_MODEL, D_MODEL), lambda l, j: (l, 0, j)),
            pl.BlockSpec((1, 1, D_MODEL), lambda l, j: (l, 0, j)),
        ],
        out_specs=pl.BlockSpec((1, MOD_ROWS, D_MODEL), lambda l, j: (l, 0, j)),
        out_shape=jax.ShapeDtypeStruct((DEPTH, MOD_ROWS, 3 * D_MODEL), F32),
        compiler_params=pltpu.CompilerParams(
            dimension_semantics=("arbitrary", "arbitrary"),
            vmem_limit_bytes=VMEM_LIMIT_BYTES),
        name="adaln_mod",
    )(cvecs, w_ada, b_ada.reshape(DEPTH, 1, 3 * D_MODEL))


def _layers_kernel(*refs, L, latent):
    if latent:
        (sink_ref, x_ref, mod_ref, nw_ref, win_ref, dw_ref, vec_ref, pw_ref, pbd_ref, wout_ref,
         fnw_ref, cos_ref, sin_ref, ck_ref, cv_ref,
         o_ref,
         gpad_ref, cpad_ref, gate_ref, qm_ref, kk_ref, vv_ref, mix_ref, s_ref, p_ref, es_ref,
         ckk_ref, cvv_ref) = refs
    else:
        (sink_ref, x_ref, mod_ref, nw_ref, win_ref, dw_ref, vec_ref, pw_ref, pbd_ref, wout_ref,
         fnw_ref,
         o_ref, nk_ref, nv_ref,
         gpad_ref, cpad_ref, gate_ref, qm_ref, kk_ref, vv_ref, mix_ref, s_ref, p_ref, es_ref) = refs

    l = pl.program_id(1)
    n_chunks = L // ROW_CHUNK
    kpad = BLOCK if latent else 0

    @pl.when(l == 0)
    def _():
        o_ref[0] = x_ref[0]

    for c in range(CONV_W // LANES):
        gpad_ref[c, 0:CONV_PAD, :] = jnp.zeros((CONV_PAD, LANES), F32)
        gpad_ref[c, CONV_PAD + L:CONV_PAD + L + CONV_PAD, :] = jnp.zeros((CONV_PAD, LANES), F32)
        cpad_ref[c, 0:POOL_PAD, :] = jnp.zeros((POOL_PAD, LANES), F32)
        cpad_ref[c, POOL_PAD + L:POOL_PAD + L + POOL_PAD, :] = jnp.zeros((POOL_PAD, LANES), F32)
    if latent:
        for g in range(N_KV_HEADS):
            for ref in (kk_ref, vv_ref):
                ref[g, 0:kpad, :] = jnp.zeros((kpad, LANES), BF16)
                ref[g, kpad + L:kpad + L + kpad, :] = jnp.zeros((kpad, LANES), BF16)
        ck0, ck1 = _dup_halves(ck_ref[0, 0])
        cv0, cv1 = _with_ones(cv_ref[0, 0]) if ROW_SUMS_ON_MXU[latent] else _dup_halves(cv_ref[0, 0])
        ckk_ref[0] = ck0.astype(BF16)
        ckk_ref[1] = ck1.astype(BF16)
        cvv_ref[0] = cv0.astype(BF16)
        cvv_ref[1] = cv1.astype(BF16)

    shift = mod_ref[0, 0, 0:1, :]
    scl = mod_ref[0, 0, 1:2, :]
    gate = mod_ref[0, 0, 2:3, :]

    def proj_chunk(c):
        r0 = c * ROW_CHUNK
        rows = pl.ds(r0, ROW_CHUNK)
        x = o_ref[0, rows, :]
        h = (_rms(x) * nw_ref[0]) * (1.0 + scl) + shift
        hb = h.astype(BF16)

        ua = jnp.dot(hb, win_ref[0, :, OFF_A:OFF_Q], preferred_element_type=F32)
        glu = ua[:, 0:CONV_W] * jax.nn.sigmoid(ua[:, CONV_W:2 * CONV_W])
        for c in range(CONV_W // LANES):
            gpad_ref[c, pl.ds(r0 + CONV_PAD, ROW_CHUNK), :] = glu[:, c * LANES:(c + 1) * LANES]
        gate_ref[rows, 0:CONV_W] = _silu(ua[:, 2 * CONV_W:3 * CONV_W])

        uq = jnp.dot(hb, win_ref[0, :, OFF_Q:OFF_KV], preferred_element_type=F32)
        ukv = jnp.dot(hb, win_ref[0, :, OFF_KV:OFF_B_GATE], preferred_element_type=F32)
        k = ukv[:, 0:KV_W]
        v = ukv[:, KV_W:2 * KV_W]
        if latent:
            cos = cos_ref[rows, :]
            sin = sin_ref[rows, :]
            k = _rope(k, cos, sin)
        else:
            nk_ref[0, 0, rows, :] = k
            nv_ref[0, 0, rows, :] = v
        lo = _lane_iota((ROW_CHUNK, LANES)) < HEAD_DIM
        scale = HEAD_DIM ** -0.5 * LOG2E
        for p in range(ATTN_W // LANES):
            qp = uq[:, p * LANES:(p + 1) * LANES]
            if latent:
                qp = _rope(qp, cos, sin)
            qp = qp * scale
            g, c = divmod(2 * p, GQA_GROUP)
            qm_ref[g, rows, c * LANES:(c + 1) * LANES] = jnp.where(lo, qp, 0.0).astype(BF16)
            qm_ref[g, rows, (c + 1) * LANES:(c + 2) * LANES] = jnp.where(lo, 0.0, qp).astype(BF16)
        k0, k1 = _dup_halves(k)
        v0, v1 = _with_ones(v) if ROW_SUMS_ON_MXU[latent] else _dup_halves(v)
        krows = pl.ds(r0 + kpad, ROW_CHUNK)
        kk_ref[0, krows, :] = k0.astype(BF16)
        kk_ref[1, krows, :] = k1.astype(BF16)
        vv_ref[0, krows, :] = v0.astype(BF16)
        vv_ref[1, krows, :] = v1.astype(BF16)

        ub = jnp.dot(hb, win_ref[0, :, OFF_B_GATE:OFF_C], preferred_element_type=F32)
        gate_ref[rows, CONV_W:CONV_W + ATTN_W] = _silu(ub)

        uc = jnp.dot(hb, win_ref[0, :, OFF_C:IN_W], preferred_element_type=F32)
        for c in range(POOL_W // LANES):
            cpad_ref[c, pl.ds(r0 + POOL_PAD, ROW_CHUNK), :] = uc[:, c * LANES:(c + 1) * LANES]
        gate_ref[rows, CONV_W + ATTN_W:D_MODEL] = _silu(uc[:, POOL_W:2 * POOL_W])

    conv_b = vec_ref[0, 0:1, :]
    ln_g = vec_ref[0, 1:2, :]
    ln_b = vec_ref[0, 2:3, :]
    pool_scale = vec_ref[0, 3:4, :]

    def conv_rows(r0):
        rows = pl.ds(r0, BLOCK)
        pieces = []
        for t in range(BLOCK // CONV_TILE):
            base = r0 + t * CONV_TILE + (CONV_PAD - CONV_K // 2)
            halves = []
            for c in range(CONV_W // LANES):
                acc = jnp.zeros((CONV_TILE, LANES), F32)
                for kk in range(CONV_K):
                    acc = acc + (gpad_ref[c, pl.ds(base + kk, CONV_TILE), :]
                                 * dw_ref[0, kk:kk + 1, c * LANES:(c + 1) * LANES])
                halves.append(acc)
            y = jnp.concatenate(halves, axis=1) + conv_b
            mu = jnp.mean(y, axis=-1, keepdims=True)
            yc = y - mu
            var = jnp.mean(yc * yc, axis=-1, keepdims=True)
            yn = (yc * lax.rsqrt(var + EPS)) * ln_g + ln_b
            pieces.append(_silu(yn).astype(BF16))
        z = jnp.concatenate(pieces, axis=0)
        a = jnp.dot(z, pw_ref[0], preferred_element_type=F32)
        mix_ref[rows, 0:CONV_W] = (a * gate_ref[rows, 0:CONV_W]).astype(BF16)

    def pool_rows(r0):
        rows = pl.ds(r0, BLOCK)
        t = r0 + lax.broadcasted_iota(jnp.int32, (BLOCK, LANES), 0)
        first_group = _lane_iota((BLOCK, LANES)) < POOL_GROUP_W
        deltas = []
        for c in range(POOL_W // LANES):

            def tok(off, c=c):
                return cpad_ref[c, pl.ds(r0 + POOL_PAD + off, BLOCK), :]

            centre = tok(0)
            half_a, half_b = POOL_WINDOWS[2 * c] // 2, POOL_WINDOWS[2 * c + 1] // 2
            sum_a = centre
            for off in list(range(-half_a, 0)) + list(range(1, half_a)):
                sum_a = sum_a + tok(off)
            sum_b = sum_a
            for off in list(range(-half_b, -half_a)) + list(range(half_a, half_b)):
                sum_b = sum_b + tok(off)
            half_w = jnp.where(first_group, half_a, half_b)
            cnt = jnp.minimum(t + half_w, L) - jnp.maximum(t - half_w, 0)
            mean = jnp.where(first_group, sum_a, sum_b) / cnt.astype(F32)
            deltas.append((mean - centre).astype(BF16))
        d = jnp.concatenate(deltas, axis=1)
        c_out = jnp.dot(d, pbd_ref[0], preferred_element_type=F32) * pool_scale
        c0 = CONV_W + ATTN_W
        mix_ref[rows, c0:D_MODEL] = (c_out * gate_ref[rows, c0:D_MODEL]).astype(BF16)

    n_blocks = L // BLOCK
    n_local = 3 * BLOCK if latent else L
    lo_q = _lane_iota((BLOCK, LANES)) < HEAD_DIM
    contract_last = (((1,), (1,)), ((), ()))

    def fold(pieces, op):
        acc = None
        for piece in pieces:
            for c in range(piece.shape[1] // LANES):
                slab = piece[:, c * LANES:(c + 1) * LANES]
                acc = slab if acc is None else op(acc, slab)
        return acc

    def block_rows(i):
        q0 = i * BLOCK if isinstance(i, int) else pl.multiple_of(i * BLOCK, BLOCK)
        return q0, (pl.ds(q0, n_local) if latent else pl.ds(0, n_local))

    def scores(i, g):
        q0, local_rows = block_rows(i)
        qm = jnp.concatenate([qm_ref[g, pl.ds(q0, BLOCK), j * LANES:(j + 1) * LANES]
                              for j in range(GQA_GROUP)], axis=0)
        s_ref[g, :, 0:n_local] = lax.dot_general(qm, kk_ref[g, local_rows, :], contract_last,
                                                 preferred_element_type=F32)
        if latent:
            s_ref[g, :, n_local:] = lax.dot_general(qm, ckk_ref[g], contract_last,
                                                    preferred_element_type=F32)

    def softmax_pv(i, g):
        q0, local_rows = block_rows(i)
        qrows = pl.ds(q0, BLOCK)
        if latent:
            jq = lax.broadcasted_iota(jnp.int32, (SOFTMAX_ROWS, BLOCK), 0)
            jk = lax.broadcasted_iota(jnp.int32, (SOFTMAX_ROWS, BLOCK), 1)
            prev_off = jnp.where(i > 0, 0, BLOCK)
            next_off = jnp.where(i < n_blocks - 1, 0, BLOCK)
        for j in range(GQA_GROUP):
            sink = sink_ref[l, g * GQA_GROUP + j] * LOG2E
            for r in range(BLOCK // SOFTMAX_ROWS):
                rr = slice(j * BLOCK + r * SOFTMAX_ROWS, j * BLOCK + (r + 1) * SOFTMAX_ROWS)
                sj = s_ref[g, rr, :]
                if latent:
                    prev_ok = jk >= jq + (prev_off + r * SOFTMAX_ROWS)
                    next_ok = jk <= jq + (r * SOFTMAX_ROWS - next_off)
                    pieces = [jnp.where(prev_ok, sj[:, 0:BLOCK], NEG_INF),
                              sj[:, BLOCK:2 * BLOCK],
                              jnp.where(next_ok, sj[:, 2 * BLOCK:3 * BLOCK], NEG_INF),
                              sj[:, n_local:]]
                else:
                    pieces = [sj]
                m = jnp.maximum(jnp.max(fold(pieces, jnp.maximum), axis=-1, keepdims=True), sink)
                e = [jnp.exp2(piece - m) for piece in pieces]
                p_ref[g, rr, :] = jnp.concatenate(e, axis=1).astype(BF16)
                tail = jnp.exp2(sink - m)
                if not ROW_SUMS_ON_MXU[latent]:
                    tail = tail + jnp.sum(fold(e, jnp.add), axis=-1, keepdims=True)
                es_ref[g, rr, :] = jnp.broadcast_to(tail, (SOFTMAX_ROWS, LANES))
        o = jnp.dot(p_ref[g, :, 0:n_local], vv_ref[g, local_rows, :],
                    preferred_element_type=F32)
        if latent:
            o = o + jnp.dot(p_ref[g, :, n_local:], cvv_ref[g], preferred_element_type=F32)
        pairs = []
        for p in range(GQA_GROUP // 2):
            rows_a = slice((2 * p) * BLOCK, (2 * p + 1) * BLOCK)
            rows_b = slice((2 * p + 1) * BLOCK, (2 * p + 2) * BLOCK)
            o_a = o[rows_a]
            o_b = o[rows_b]
            if ROW_SUMS_ON_MXU[latent]:
                inv_a = 1.0 / (o_a + es_ref[g, rows_a, :])
                inv_b = 1.0 / (o_b + es_ref[g, rows_b, :])
                pairs.append(jnp.where(lo_q, o_a * pltpu.roll(inv_a, HEAD_DIM, axis=1),
                                       pltpu.roll(o_b, HEAD_DIM, axis=1) * inv_b))
            else:
                pairs.append(jnp.where(lo_q, o_a / es_ref[g, rows_a, :], o_b / es_ref[g, rows_b, :]))
        c0 = CONV_W + g * (ATTN_W // N_KV_HEADS)
        cols = slice(c0, c0 + ATTN_W // N_KV_HEADS)
        mix_ref[qrows, cols] = (jnp.concatenate(pairs, axis=1) * gate_ref[qrows, cols]).astype(BF16)

    def attn_block(i, carry):
        scores(i, 1)
        softmax_pv(i, 0)
        scores(jnp.minimum(i + 1, n_blocks - 1), 0)
        softmax_pv(i, 1)
        if n_chunks == 1:
            r0 = pl.multiple_of(i * BLOCK, BLOCK)
            conv_rows(r0)
            pool_rows(r0)
        return carry

    def out_chunk(c):
        rows = pl.ds(c * ROW_CHUNK, ROW_CHUNK)
        y = jnp.dot(mix_ref[rows, :], wout_ref[0], preferred_element_type=F32)
        o_ref[0, rows, :] = o_ref[0, rows, :] + gate * y

    def vpu_branches(c):
        for b in range(ROW_CHUNK // BLOCK):
            conv_rows(c * ROW_CHUNK + b * BLOCK)
            pool_rows(c * ROW_CHUNK + b * BLOCK)

    for c in range(n_chunks):
        proj_chunk(c)
        if c > 0:
            vpu_branches(c - 1)
    scores(0, 0)
    lax.fori_loop(0, n_blocks, attn_block, 0, unroll=min(n_blocks, ATTN_UNROLL))
    if n_chunks == 1:
        out_chunk(0)
    else:
        last_blocks = [(n_chunks - 1) * ROW_CHUNK + b * BLOCK for b in range(ROW_CHUNK // BLOCK)]
        for c in range(n_chunks - 1):
            out_chunk(c)
            if last_blocks:
                r0 = last_blocks.pop(0)
                conv_rows(r0)
                pool_rows(r0)
        for r0 in last_blocks:
            conv_rows(r0)
            pool_rows(r0)
        out_chunk(n_chunks - 1)


    @pl.when(l == DEPTH - 1)
    def _():
        def final_chunk(i, carry):
            r0 = pl.multiple_of(i * ROW_CHUNK, ROW_CHUNK)
            rows = pl.ds(r0, ROW_CHUNK)
            o_ref[0, rows, :] = _rms(o_ref[0, rows, :]) * fnw_ref[...]
            return carry

        lax.fori_loop(0, n_chunks, final_chunk, 0)


def _mixer_layers(x, mods, mod_row, weights, rope=None, cache=None):
    S, L, _ = x.shape
    latent = cache is not None
    sink, norm_w, w_in, conv_dw, vecs, conv_pw, pool_bd, w_out, final_norm_w = weights
    kpad = BLOCK if latent else 0
    n_keys = 3 * BLOCK + cache[0].shape[2] if latent else L

    per_layer = lambda *tail: (lambda s, l: (l,) + tail)
    in_specs = [
        pl.BlockSpec(memory_space=pltpu.SMEM),
        pl.BlockSpec((1, L, D_MODEL), lambda s, l: (s, 0, 0)),
        pl.BlockSpec((1, 1, 3, D_MODEL), lambda s, l: (l, mod_row(s), 0, 0)),
        pl.BlockSpec((1, 1, D_MODEL), per_layer(0, 0)),
        pl.BlockSpec((1, D_MODEL, IN_W), per_layer(0, 0)),
        pl.BlockSpec((1, 32, CONV_W), per_layer(0, 0)),
        pl.BlockSpec((1, 8, CONV_W), per_layer(0, 0)),
        pl.BlockSpec((1, CONV_W, CONV_W), per_layer(0, 0)),
        pl.BlockSpec((1, POOL_W, POOL_W), per_layer(0, 0)),
        pl.BlockSpec((1, D_MODEL, D_MODEL), per_layer(0, 0)),
        pl.BlockSpec((1, D_MODEL), lambda s, l: (0, 0)),
    ]
    args = [sink, x, mods, norm_w, w_in, conv_dw, vecs, conv_pw, pool_bd, w_out, final_norm_w]
    out_specs = [pl.BlockSpec((1, L, D_MODEL), lambda s, l: (s, 0, 0))]
    out_shape = [jax.ShapeDtypeStruct((S, L, D_MODEL), F32)]
    scratch = [
        pltpu.VMEM((CONV_W // LANES, L + 2 * CONV_PAD, LANES), F32),
        pltpu.VMEM((POOL_W // LANES, L + 2 * POOL_PAD, LANES), F32),
        pltpu.VMEM((L, D_MODEL), F32),
        pltpu.VMEM((N_KV_HEADS, L, ATTN_W), BF16),
        pltpu.VMEM((N_KV_HEADS, L + 2 * kpad, LANES), BF16),
        pltpu.VMEM((N_KV_HEADS, L + 2 * kpad, LANES), BF16),
        pltpu.VMEM((L, D_MODEL), BF16),
        pltpu.VMEM((N_KV_HEADS, GQA_GROUP * BLOCK, n_keys), F32),
        pltpu.VMEM((N_KV_HEADS, GQA_GROUP * BLOCK, n_keys), BF16),
        pltpu.VMEM((N_KV_HEADS, GQA_GROUP * BLOCK, LANES), F32),
    ]
    if latent:
        cos, sin = rope
        ck, cv = cache
        past = ck.shape[2]
        in_specs += [
            pl.BlockSpec((L, LANES), lambda s, l: (0, 0)),
            pl.BlockSpec((L, LANES), lambda s, l: (0, 0)),
            pl.BlockSpec((1, 1, past, KV_W), lambda s, l: (s, l, 0, 0)),
            pl.BlockSpec((1, 1, past, KV_W), lambda s, l: (s, l, 0, 0)),
        ]
        args += [cos, sin, ck, cv]
        scratch += [pltpu.VMEM((N_KV_HEADS, past, LANES), BF16),
                    pltpu.VMEM((N_KV_HEADS, past, LANES), BF16)]
    else:
        out_specs += [pl.BlockSpec((1, 1, L, KV_W), lambda s, l: (s, l, 0, 0))] * 2
        out_shape += [jax.ShapeDtypeStruct((S, DEPTH, L, KV_W), F32)] * 2

    return pl.pallas_call(
        functools.partial(_layers_kernel, L=L, latent=latent),
        grid=(S, DEPTH),
        in_specs=in_specs,
        out_specs=out_specs,
        out_shape=out_shape,
        scratch_shapes=scratch,
        compiler_params=pltpu.CompilerParams(
            dimension_semantics=("arbitrary", "arbitrary"),
            vmem_limit_bytes=VMEM_LIMIT_BYTES),
        name="latent_layers" if latent else "context_layers",
    )(*args)


def _rope_tables(L):
    quarter = HEAD_DIM // 4
    t = jnp.arange(L)
    rows = (t // GRID_W).astype(F32)
    cols = (t % GRID_W).astype(F32)
    freqs = ROPE_BASE ** (-jnp.arange(quarter, dtype=F32) / quarter)
    ang_r = rows[:, None] * freqs[None, :]
    ang_c = cols[:, None] * freqs[None, :]
    cos = jnp.concatenate([jnp.cos(ang_r)] * 2 + [jnp.cos(ang_c)] * 2, axis=-1)
    sin = jnp.concatenate([-jnp.sin(ang_r), jnp.sin(ang_r), -jnp.sin(ang_c), jnp.sin(ang_c)], axis=-1)
    return jnp.tile(cos, (1, 2)), jnp.tile(sin, (1, 2))


def kernel(x_prompt, x_sample, c, cache_k, cache_v, c_ctx, w_ada, b_ada, norm_w, w_in, conv_dw, conv_b,
           conv_ln_g, conv_ln_b, conv_pw, attn_sink, pool_w, pool_scale, w_out, final_norm_w):
    n_ctx, ctx_len, _ = x_prompt.shape
    n_lat, lat_len, _ = x_sample.shape
    past = cache_k.shape[2]
    assert n_lat + 1 <= MOD_ROWS

    cvecs = jnp.zeros((MOD_ROWS, D_MODEL), F32).at[:n_lat].set(c).at[n_lat].set(c_ctx)
    mods = _adaln(cvecs, w_ada, b_ada).reshape(DEPTH, MOD_ROWS, 3, D_MODEL)

    n_groups = len(POOL_WINDOWS)
    eye = jnp.eye(n_groups, dtype=F32)
    pool_bd = (pool_w[:, :, :, None, :] * eye[None, :, None, :, None]).reshape(DEPTH, POOL_W, POOL_W)
    vecs = jnp.zeros((DEPTH, 8, CONV_W), F32)
    vecs = vecs.at[:, 0].set(conv_b).at[:, 1].set(conv_ln_g).at[:, 2].set(conv_ln_b).at[:, 3].set(pool_scale)
    weights = (
        attn_sink,
        norm_w.reshape(DEPTH, 1, D_MODEL),
        w_in.astype(BF16),
        jnp.pad(conv_dw, ((0, 0), (0, 32 - CONV_K), (0, 0))),
        vecs,
        conv_pw.astype(BF16),
        pool_bd.astype(BF16),
        w_out.astype(BF16),
        final_norm_w.reshape(1, D_MODEL),
    )

    y_prompt, new_k, new_v = _mixer_layers(x_prompt, mods, lambda s: n_lat, weights)
    new_cache_k = new_k.reshape(n_ctx, DEPTH, ctx_len, N_KV_HEADS, HEAD_DIM)
    new_cache_v = new_v.reshape(n_ctx, DEPTH, ctx_len, N_KV_HEADS, HEAD_DIM)

    (y_sample,) = _mixer_layers(
        x_sample, mods, lambda s: s, weights,
        rope=_rope_tables(lat_len),
        cache=(cache_k.reshape(n_lat, DEPTH, past, KV_W), cache_v.reshape(n_lat, DEPTH, past, KV_W)))
    return (y_prompt, y_sample, new_cache_k, new_cache_v)
```

```python
import functools

import jax
import jax.numpy as jnp
from jax import lax
from jax.experimental import pallas as pl
from jax.experimental.pallas import tpu as pltpu

D_MODEL = 1024
DEPTH = 4
GRID_W = 64
CONV_W = 256
ATTN_W = 512
POOL_W = 256
HEAD_DIM = 64
N_HEADS = 8
N_KV_HEADS = 2
GQA_GROUP = 4
KV_W = 128
WINDOW = 128
BLOCK = 128
CONV_K = 31
POOL_WINDOWS = (2, 4, 8, 16)
POOL_GROUP_W = 64
ROPE_BASE = 10000.0
EPS = 1e-6
NEG_INF = -1e30
LOG2E = 1.4426950408889634

OFF_A = 0
OFF_Q = 3 * CONV_W
OFF_KV = OFF_Q + ATTN_W
OFF_B_GATE = OFF_KV + 2 * KV_W
OFF_C = OFF_B_GATE + ATTN_W
IN_W = OFF_C + 2 * POOL_W

LANES = 128
CONV_PAD = 16
POOL_PAD = 8
ROW_CHUNK = 256
CONV_TILE = 32
ATTN_UNROLL = 2
CTX_SEQS_PER_STEP = 4
ROW_SUMS_ON_MXU = {True: True, False: False}
SOFTMAX_ROWS = 32
MOD_ROWS = 16
VMEM_LIMIT_BYTES = 56 * 1024 * 1024

F32 = jnp.float32
BF16 = jnp.bfloat16


def _silu(x):
    return x * jax.nn.sigmoid(x)


def _lane_iota(shape):
    return lax.broadcasted_iota(jnp.int32, shape, 1)


def _dup_halves(x):
    r = pltpu.roll(x, HEAD_DIM, axis=1)
    lo = _lane_iota(x.shape) < HEAD_DIM
    return jnp.where(lo, x, r), jnp.where(lo, r, x)


def _with_ones(x):
    r = pltpu.roll(x, HEAD_DIM, axis=1)
    lo = _lane_iota(x.shape) < HEAD_DIM
    return jnp.where(lo, x, 1.0), jnp.where(lo, r, 1.0)


def _rope(x, cos, sin):
    quarter = HEAD_DIM // 4
    fwd = pltpu.roll(x, LANES - quarter, axis=1)
    bwd = pltpu.roll(x, quarter, axis=1)
    first = (_lane_iota(x.shape) & (2 * quarter - 1)) < quarter
    return x * cos + jnp.where(first, fwd, bwd) * sin


def _rms(x):
    return x * lax.rsqrt(jnp.mean(x * x, axis=-1, keepdims=True) + EPS)


def _adaln_kernel(c_ref, w_ref, b_ref, o_ref):
    s = _silu(c_ref[...])
    o_ref[0] = jnp.dot(s.astype(BF16), w_ref[0].astype(BF16), preferred_element_type=F32) + b_ref[0]


def _adaln(cvecs, w_ada, b_ada):
    n_col = 3
    return pl.pallas_call(
        _adaln_kernel,
        grid=(DEPTH, n_col),
        in_specs=[
            pl.BlockSpec((MOD_ROWS, D_MODEL), lambda l, j: (0, 0)),
            pl.BlockSpec((1, D_MODEL, D_MODEL), lambda l, j: (l, 0, j)),
            pl.BlockSpec((1, 1, D_MODEL), lambda l, j: (l, 0, j)),
        ],
        out_specs=pl.BlockSpec((1, MOD_ROWS, D_MODEL), lambda l, j: (l, 0, j)),
        out_shape=jax.ShapeDtypeStruct((DEPTH, MOD_ROWS, 3 * D_MODEL), F32),
        compiler_params=pltpu.CompilerParams(
            dimension_semantics=("arbitrary", "arbitrary"),
            vmem_limit_bytes=VMEM_LIMIT_BYTES),
        name="adaln_mod",
    )(cvecs, w_ada, b_ada.reshape(DEPTH, 1, 3 * D_MODEL))


def _layers_kernel(*refs, L, n_seq, latent):
    if latent:
        (sink_ref, x_ref, mod_ref, nw_ref, win_ref, dw_ref, vec_ref, pw_ref, pbd_ref, wout_ref,
         fnw_ref, cos_ref, sin_ref, ck_ref, cv_ref,
         o_ref,
         gpad_ref, cpad_ref, gate_a_ref, gate_b_ref, gate_c_ref, qm_ref, kk_ref, vv_ref,
         mix_a_ref, mix_b_ref, mix_c_ref, s_ref, p_ref, es_ref,
         ckk_ref, cvv_ref) = refs
    else:
        (sink_ref, x_ref, mod_ref, nw_ref, win_ref, dw_ref, vec_ref, pw_ref, pbd_ref, wout_ref,
         fnw_ref,
         o_ref, nk_ref, nv_ref,
         gpad_ref, cpad_ref, gate_a_ref, gate_b_ref, gate_c_ref, qm_ref, kk_ref, vv_ref,
         mix_a_ref, mix_b_ref, mix_c_ref, s_ref, p_ref, es_ref) = refs

    l = pl.program_id(1)
    n_chunks = n_seq * L // ROW_CHUNK
    kpad = BLOCK if latent else 0
    assert not latent or n_seq == 1

    def locate(r0):
        if isinstance(r0, int):
            return divmod(r0, L)
        if n_seq == 1:
            return 0, pl.multiple_of(r0, BLOCK)
        seq = lax.div(r0, jnp.int32(L))
        return seq, pl.multiple_of(r0 - seq * L, BLOCK)

    @pl.when(l == 0)
    def _():
        o_ref[...] = x_ref[...]

    for c in range(CONV_W // LANES):
        for s in range(n_seq):
            gpad_ref[c, s, 0:CONV_PAD, :] = jnp.zeros((CONV_PAD, LANES), F32)
            gpad_ref[c, s, CONV_PAD + L:CONV_PAD + L + CONV_PAD, :] = jnp.zeros((CONV_PAD, LANES), F32)
            cpad_ref[c, s, 0:POOL_PAD, :] = jnp.zeros((POOL_PAD, LANES), F32)
            cpad_ref[c, s, POOL_PAD + L:POOL_PAD + L + POOL_PAD, :] = jnp.zeros((POOL_PAD, LANES), F32)
    if latent:
        for g in range(N_KV_HEADS):
            for ref in (kk_ref, vv_ref):
                ref[g, 0:kpad, :] = jnp.zeros((kpad, LANES), BF16)
                ref[g, kpad + L:kpad + L + kpad, :] = jnp.zeros((kpad, LANES), BF16)
        ck0, ck1 = _dup_halves(ck_ref[0, 0])
        cv0, cv1 = _with_ones(cv_ref[0, 0]) if ROW_SUMS_ON_MXU[latent] else _dup_halves(cv_ref[0, 0])
        ckk_ref[0] = ck0.astype(BF16)
        ckk_ref[1] = ck1.astype(BF16)
        cvv_ref[0] = cv0.astype(BF16)
        cvv_ref[1] = cv1.astype(BF16)

    shift = mod_ref[0, 0, 0:1, :]
    scl = mod_ref[0, 0, 1:2, :]
    gate = mod_ref[0, 0, 2:3, :]

    def proj_chunk(c):
        r0 = c * ROW_CHUNK if isinstance(c, int) else pl.multiple_of(c * ROW_CHUNK, ROW_CHUNK)
        seq, rs = locate(r0)
        rows = pl.ds(r0, ROW_CHUNK)
        seq_rows = pl.ds(rs, ROW_CHUNK)
        x = o_ref[seq, seq_rows, :]
        h = (_rms(x) * nw_ref[0]) * (1.0 + scl) + shift
        hb = h.astype(BF16)

        ua = jnp.dot(hb, win_ref[0, :, OFF_A:OFF_Q], preferred_element_type=F32)
        glu = ua[:, 0:CONV_W] * jax.nn.sigmoid(ua[:, CONV_W:2 * CONV_W])
        for c in range(CONV_W // LANES):
            gpad_ref[c, seq, pl.ds(rs + CONV_PAD, ROW_CHUNK), :] = glu[:, c * LANES:(c + 1) * LANES]
        gate_a_ref[rows, :] = _silu(ua[:, 2 * CONV_W:3 * CONV_W])

        uq = jnp.dot(hb, win_ref[0, :, OFF_Q:OFF_KV], preferred_element_type=F32)
        ukv = jnp.dot(hb, win_ref[0, :, OFF_KV:OFF_B_GATE], preferred_element_type=F32)
        k = ukv[:, 0:KV_W]
        v = ukv[:, KV_W:2 * KV_W]
        if latent:
            cos = cos_ref[seq_rows, :]
            sin = sin_ref[seq_rows, :]
            k = _rope(k, cos, sin)
        else:
            nk_ref[seq, 0, seq_rows, :] = k
            nv_ref[seq, 0, seq_rows, :] = v
        lo = _lane_iota((ROW_CHUNK, LANES)) < HEAD_DIM
        scale = HEAD_DIM ** -0.5 * LOG2E
        for p in range(ATTN_W // LANES):
            qp = uq[:, p * LANES:(p + 1) * LANES]
            if latent:
                qp = _rope(qp, cos, sin)
            qp = qp * scale
            g, c = divmod(2 * p, GQA_GROUP)
            qm_ref[g, rows, c * LANES:(c + 1) * LANES] = jnp.where(lo, qp, 0.0).astype(BF16)
            qm_ref[g, rows, (c + 1) * LANES:(c + 2) * LANES] = jnp.where(lo, 0.0, qp).astype(BF16)
        k0, k1 = _dup_halves(k)
        v0, v1 = _with_ones(v) if ROW_SUMS_ON_MXU[latent] else _dup_halves(v)
        krows = pl.ds(r0 + kpad, ROW_CHUNK)
        kk_ref[0, krows, :] = k0.astype(BF16)
        kk_ref[1, krows, :] = k1.astype(BF16)
        vv_ref[0, krows, :] = v0.astype(BF16)
        vv_ref[1, krows, :] = v1.astype(BF16)

        ub = jnp.dot(hb, win_ref[0, :, OFF_B_GATE:OFF_C], preferred_element_type=F32)
        gate_b_ref[rows, :] = _silu(ub)

        uc = jnp.dot(hb, win_ref[0, :, OFF_C:IN_W], preferred_element_type=F32)
        for c in range(POOL_W // LANES):
            cpad_ref[c, seq, pl.ds(rs + POOL_PAD, ROW_CHUNK), :] = uc[:, c * LANES:(c + 1) * LANES]
        gate_c_ref[rows, :] = _silu(uc[:, POOL_W:2 * POOL_W])

    conv_b = vec_ref[0, 0:1, :]
    ln_g = vec_ref[0, 1:2, :]
    ln_b = vec_ref[0, 2:3, :]
    pool_scale = vec_ref[0, 3:4, :]

    def conv_rows(r0):
        seq, rs = locate(r0)
        rows = pl.ds(r0, BLOCK)
        pieces = []
        for t in range(BLOCK // CONV_TILE):
            base = rs + t * CONV_TILE + (CONV_PAD - CONV_K // 2)
            halves = []
            for c in range(CONV_W // LANES):
                acc = jnp.zeros((CONV_TILE, LANES), F32)
                for kk in range(CONV_K):
                    acc = acc + (gpad_ref[c, seq, pl.ds(base + kk, CONV_TILE), :]
                                 * dw_ref[0, kk:kk + 1, c * LANES:(c + 1) * LANES])
                halves.append(acc)
            y = jnp.concatenate(halves, axis=1) + conv_b
            mu = jnp.mean(y, axis=-1, keepdims=True)
            yc = y - mu
            var = jnp.mean(yc * yc, axis=-1, keepdims=True)
            yn = (yc * lax.rsqrt(var + EPS)) * ln_g + ln_b
            pieces.append(_silu(yn).astype(BF16))
        z = jnp.concatenate(pieces, axis=0)
        a = jnp.dot(z, pw_ref[0], preferred_element_type=F32)
        mix_a_ref[rows, :] = (a * gate_a_ref[rows, :]).astype(BF16)

    def pool_rows(r0):
        seq, rs = locate(r0)
        rows = pl.ds(r0, BLOCK)
        t = rs + lax.broadcasted_iota(jnp.int32, (BLOCK, LANES), 0)
        first_group = _lane_iota((BLOCK, LANES)) < POOL_GROUP_W
        deltas = []
        for c in range(POOL_W // LANES):

            def tok(off, c=c):
                return cpad_ref[c, seq, pl.ds(rs + POOL_PAD + off, BLOCK), :]

            centre = tok(0)
            half_a, half_b = POOL_WINDOWS[2 * c] // 2, POOL_WINDOWS[2 * c + 1] // 2
            sum_a = centre
            for off in list(range(-half_a, 0)) + list(range(1, half_a)):
                sum_a = sum_a + tok(off)
            sum_b = sum_a
            for off in list(range(-half_b, -half_a)) + list(range(half_a, half_b)):
                sum_b = sum_b + tok(off)
            half_w = jnp.where(first_group, half_a, half_b)
            cnt = jnp.minimum(t + half_w, L) - jnp.maximum(t - half_w, 0)
            mean = jnp.where(first_group, sum_a, sum_b) / cnt.astype(F32)
            deltas.append((mean - centre).astype(BF16))
        d = jnp.concatenate(deltas, axis=1)
        c_out = jnp.dot(d, pbd_ref[0], preferred_element_type=F32) * pool_scale
        mix_c_ref[rows, :] = (c_out * gate_c_ref[rows, :]).astype(BF16)

    n_blocks = n_seq * L // BLOCK
    seq_blocks = L // BLOCK
    n_local = 3 * BLOCK if latent else L
    lo_q = _lane_iota((BLOCK, LANES)) < HEAD_DIM
    contract_last = (((1,), (1,)), ((), ()))

    def fold(pieces, op):
        acc = None
        for piece in pieces:
            for c in range(piece.shape[1] // LANES):
                slab = piece[:, c * LANES:(c + 1) * LANES]
                acc = slab if acc is None else op(acc, slab)
        return acc

    def block_rows(i):
        if isinstance(i, int):
            q0, k0 = i * BLOCK, (i // seq_blocks) * L
        else:
            q0 = pl.multiple_of(i * BLOCK, BLOCK)
            k0 = pl.multiple_of(lax.div(i, jnp.int32(seq_blocks)) * L, L)
        return q0, (pl.ds(q0, n_local) if latent else pl.ds(k0, n_local))

    def scores(i, g):
        q0, local_rows = block_rows(i)
        qm = jnp.concatenate([qm_ref[g, pl.ds(q0, BLOCK), j * LANES:(j + 1) * LANES]
                              for j in range(GQA_GROUP)], axis=0)
        s_ref[g, :, 0:n_local] = lax.dot_general(qm, kk_ref[g, local_rows, :], contract_last,
                                                 preferred_element_type=F32)
        if latent:
            s_ref[g, :, n_local:] = lax.dot_general(qm, ckk_ref[g], contract_last,
                                                    preferred_element_type=F32)

    def softmax_pv(i, g):
        q0, local_rows = block_rows(i)
        qrows = pl.ds(q0, BLOCK)
        if latent:
            jq = lax.broadcasted_iota(jnp.int32, (SOFTMAX_ROWS, BLOCK), 0)
            jk = lax.broadcasted_iota(jnp.int32, (SOFTMAX_ROWS, BLOCK), 1)
            prev_off = jnp.where(i > 0, 0, BLOCK)
            next_off = jnp.where(i < n_blocks - 1, 0, BLOCK)
        for j in range(GQA_GROUP):
            sink = sink_ref[l, g * GQA_GROUP + j] * LOG2E
            for r in range(BLOCK // SOFTMAX_ROWS):
                rr = slice(j * BLOCK + r * SOFTMAX_ROWS, j * BLOCK + (r + 1) * SOFTMAX_ROWS)
                sj = s_ref[g, rr, :]
                if latent:
                    prev_ok = jk >= jq + (prev_off + r * SOFTMAX_ROWS)
                    next_ok = jk <= jq + (r * SOFTMAX_ROWS - next_off)
                    pieces = [jnp.where(prev_ok, sj[:, 0:BLOCK], NEG_INF),
                              sj[:, BLOCK:2 * BLOCK],
                              jnp.where(next_ok, sj[:, 2 * BLOCK:3 * BLOCK], NEG_INF),
                              sj[:, n_local:]]
                else:
                    pieces = [sj]
                m = jnp.maximum(jnp.max(fold(pieces, jnp.maximum), axis=-1, keepdims=True), sink)
                e = [jnp.exp2(piece - m) for piece in pieces]
                p_ref[g, rr, :] = jnp.concatenate(e, axis=1).astype(BF16)
                tail = jnp.exp2(sink - m)
                if not ROW_SUMS_ON_MXU[latent]:
                    tail = tail + jnp.sum(fold(e, jnp.add), axis=-1, keepdims=True)
                es_ref[g, rr, :] = jnp.broadcast_to(tail, (SOFTMAX_ROWS, LANES))
        o = jnp.dot(p_ref[g, :, 0:n_local], vv_ref[g, local_rows, :],
                    preferred_element_type=F32)
        if latent:
            o = o + jnp.dot(p_ref[g, :, n_local:], cvv_ref[g], preferred_element_type=F32)
        pairs = []
        for p in range(GQA_GROUP // 2):
            rows_a = slice((2 * p) * BLOCK, (2 * p + 1) * BLOCK)
            rows_b = slice((2 * p + 1) * BLOCK, (2 * p + 2) * BLOCK)
            o_a = o[rows_a]
            o_b = o[rows_b]
            if ROW_SUMS_ON_MXU[latent]:
                inv_a = 1.0 / (o_a + es_ref[g, rows_a, :])
                inv_b = 1.0 / (o_b + es_ref[g, rows_b, :])
                pairs.append(jnp.where(lo_q, o_a * pltpu.roll(inv_a, HEAD_DIM, axis=1),
                                       pltpu.roll(o_b, HEAD_DIM, axis=1) * inv_b))
            else:
                pairs.append(jnp.where(lo_q, o_a / es_ref[g, rows_a, :], o_b / es_ref[g, rows_b, :]))
        c0 = g * (ATTN_W // N_KV_HEADS)
        cols = slice(c0, c0 + ATTN_W // N_KV_HEADS)
        mix_b_ref[qrows, cols] = (jnp.concatenate(pairs, axis=1) * gate_b_ref[qrows, cols]).astype(BF16)

    def attn_block(i, carry):
        scores(i, 1)
        softmax_pv(i, 0)
        scores(jnp.minimum(i + 1, n_blocks - 1), 0)
        softmax_pv(i, 1)
        return carry

    def out_chunk(c):
        seq, rs = locate(c * ROW_CHUNK)
        seq_rows = pl.ds(rs, ROW_CHUNK)
        rows = pl.ds(c * ROW_CHUNK, ROW_CHUNK)
        y = (jnp.dot(mix_a_ref[rows, :], wout_ref[0, 0:CONV_W, :], preferred_element_type=F32)
             + jnp.dot(mix_b_ref[rows, :], wout_ref[0, CONV_W:CONV_W + ATTN_W, :], preferred_element_type=F32)
             + jnp.dot(mix_c_ref[rows, :], wout_ref[0, CONV_W + ATTN_W:D_MODEL, :], preferred_element_type=F32))
        o_ref[seq, seq_rows, :] = o_ref[seq, seq_rows, :] + gate * y

    def vpu_branches(c):
        for b in range(ROW_CHUNK // BLOCK):
            r0 = c * ROW_CHUNK + b * BLOCK
            r0 = r0 if isinstance(r0, int) else pl.multiple_of(r0, BLOCK)
            conv_rows(r0)
            pool_rows(r0)

    def proj_and_branches(c, carry):
        proj_chunk(c)
        vpu_branches(c - 1)
        return carry

    proj_chunk(0)
    lax.fori_loop(1, n_chunks, proj_and_branches, 0)
    scores(0, 0)
    lax.fori_loop(0, n_blocks, attn_block, 0, unroll=min(n_blocks, ATTN_UNROLL))
    last_blocks = [(n_chunks - 1) * ROW_CHUNK + b * BLOCK for b in range(ROW_CHUNK // BLOCK)]
    for c in range(n_chunks - 1):
        out_chunk(c)
        if last_blocks:
            r0 = last_blocks.pop(0)
            conv_rows(r0)
            pool_rows(r0)
    for r0 in last_blocks:
        conv_rows(r0)
        pool_rows(r0)
    out_chunk(n_chunks - 1)

    @pl.when(l == DEPTH - 1)
    def _():
        for c in range(n_chunks):
            seq, rs = locate(c * ROW_CHUNK)
            seq_rows = pl.ds(rs, ROW_CHUNK)
            o_ref[seq, seq_rows, :] = _rms(o_ref[seq, seq_rows, :]) * fnw_ref[...]


def _mixer_layers(x, mods, mod_row, weights, n_seq, rope=None, cache=None):
    S, L, _ = x.shape
    latent = cache is not None
    sink, norm_w, w_in, conv_dw, vecs, conv_pw, pool_bd, w_out, final_norm_w = weights
    kpad = BLOCK if latent else 0
    n_keys = 3 * BLOCK + cache[0].shape[2] if latent else L
    R = n_seq * L
    assert S % n_seq == 0 and R % ROW_CHUNK == 0 and L % BLOCK == 0 and (L % ROW_CHUNK == 0 or ROW_CHUNK % L == 0)

    per_layer = lambda *tail: (lambda s, l: (l,) + tail)
    in_specs = [
        pl.BlockSpec(memory_space=pltpu.SMEM),
        pl.BlockSpec((n_seq, L, D_MODEL), lambda s, l: (s, 0, 0)),
        pl.BlockSpec((1, 1, 3, D_MODEL), lambda s, l: (l, mod_row(s), 0, 0)),
        pl.BlockSpec((1, 1, D_MODEL), per_layer(0, 0)),
        pl.BlockSpec((1, D_MODEL, IN_W), per_layer(0, 0)),
        pl.BlockSpec((1, 32, CONV_W), per_layer(0, 0)),
        pl.BlockSpec((1, 8, CONV_W), per_layer(0, 0)),
        pl.BlockSpec((1, CONV_W, CONV_W), per_layer(0, 0)),
        pl.BlockSpec((1, POOL_W, POOL_W), per_layer(0, 0)),
        pl.BlockSpec((1, D_MODEL, D_MODEL), per_layer(0, 0)),
        pl.BlockSpec((1, D_MODEL), lambda s, l: (0, 0)),
    ]
    args = [sink, x, mods, norm_w, w_in, conv_dw, vecs, conv_pw, pool_bd, w_out, final_norm_w]
    out_specs = [pl.BlockSpec((n_seq, L, D_MODEL), lambda s, l: (s, 0, 0))]
    out_shape = [jax.ShapeDtypeStruct((S, L, D_MODEL), F32)]
    scratch = [
        pltpu.VMEM((CONV_W // LANES, n_seq, L + 2 * CONV_PAD, LANES), F32),
        pltpu.VMEM((POOL_W // LANES, n_seq, L + 2 * POOL_PAD, LANES), F32),
        pltpu.VMEM((R, CONV_W), F32),
        pltpu.VMEM((R, ATTN_W), F32),
        pltpu.VMEM((R, POOL_W), F32),
        pltpu.VMEM((N_KV_HEADS, R, ATTN_W), BF16),
        pltpu.VMEM((N_KV_HEADS, R + 2 * kpad, LANES), BF16),
        pltpu.VMEM((N_KV_HEADS, R + 2 * kpad, LANES), BF16),
        pltpu.VMEM((R, CONV_W), BF16),
        pltpu.VMEM((R, ATTN_W), BF16),
        pltpu.VMEM((R, POOL_W), BF16),
        pltpu.VMEM((N_KV_HEADS, GQA_GROUP * BLOCK, n_keys), F32),
        pltpu.VMEM((N_KV_HEADS, GQA_GROUP * BLOCK, n_keys), BF16),
        pltpu.VMEM((N_KV_HEADS, GQA_GROUP * BLOCK, LANES), F32),
    ]
    if latent:
        cos, sin = rope
        ck, cv = cache
        past = ck.shape[2]
        in_specs += [
            pl.BlockSpec((L, LANES), lambda s, l: (0, 0)),
            pl.BlockSpec((L, LANES), lambda s, l: (0, 0)),
            pl.BlockSpec((1, 1, past, KV_W), lambda s, l: (s, l, 0, 0)),
            pl.BlockSpec((1, 1, past, KV_W), lambda s, l: (s, l, 0, 0)),
        ]
        args += [cos, sin, ck, cv]
        scratch += [pltpu.VMEM((N_KV_HEADS, past, LANES), BF16),
                    pltpu.VMEM((N_KV_HEADS, past, LANES), BF16)]
    else:
        out_specs += [pl.BlockSpec((n_seq, 1, L, KV_W), lambda s, l: (s, l, 0, 0))] * 2
        out_shape += [jax.ShapeDtypeStruct((S, DEPTH, L, KV_W), F32)] * 2

    return pl.pallas_call(
        functools.partial(_layers_kernel, L=L, n_seq=n_seq, latent=latent),
        grid=(S // n_seq, DEPTH),
        in_specs=in_specs,
        out_specs=out_specs,
        out_shape=out_shape,
        scratch_shapes=scratch,
        compiler_params=pltpu.CompilerParams(
            dimension_semantics=("arbitrary", "arbitrary"),
            vmem_limit_bytes=VMEM_LIMIT_BYTES),
        name="latent_layers" if latent else "context_layers",
    )(*args)


def _rope_tables(L):
    quarter = HEAD_DIM // 4
    t = jnp.arange(L)
    rows = (t // GRID_W).astype(F32)
    cols = (t % GRID_W).astype(F32)
    freqs = ROPE_BASE ** (-jnp.arange(quarter, dtype=F32) / quarter)
    ang_r = rows[:, None] * freqs[None, :]
    ang_c = cols[:, None] * freqs[None, :]
    cos = jnp.concatenate([jnp.cos(ang_r)] * 2 + [jnp.cos(ang_c)] * 2, axis=-1)
    sin = jnp.concatenate([-jnp.sin(ang_r), jnp.sin(ang_r), -jnp.sin(ang_c), jnp.sin(ang_c)], axis=-1)
    return jnp.tile(cos, (1, 2)), jnp.tile(sin, (1, 2))


def kernel(x_prompt, x_sample, c, cache_k, cache_v, c_ctx, w_ada, b_ada, norm_w, w_in, conv_dw, conv_b,
           conv_ln_g, conv_ln_b, conv_pw, attn_sink, pool_w, pool_scale, w_out, final_norm_w):
    n_ctx, ctx_len, _ = x_prompt.shape
    n_lat, lat_len, _ = x_sample.shape
    past = cache_k.shape[2]
    assert n_lat + 1 <= MOD_ROWS

    cvecs = jnp.zeros((MOD_ROWS, D_MODEL), F32).at[:n_lat].set(c).at[n_lat].set(c_ctx)
    mods = _adaln(cvecs, w_ada, b_ada).reshape(DEPTH, MOD_ROWS, 3, D_MODEL)

    n_groups = len(POOL_WINDOWS)
    eye = jnp.eye(n_groups, dtype=F32)
    pool_bd = (pool_w[:, :, :, None, :] * eye[None, :, None, :, None]).reshape(DEPTH, POOL_W, POOL_W)
    vecs = jnp.zeros((DEPTH, 8, CONV_W), F32)
    vecs = vecs.at[:, 0].set(conv_b).at[:, 1].set(conv_ln_g).at[:, 2].set(conv_ln_b).at[:, 3].set(pool_scale)
    weights = (
        attn_sink,
        norm_w.reshape(DEPTH, 1, D_MODEL),
        w_in.astype(BF16),
        jnp.pad(conv_dw, ((0, 0), (0, 32 - CONV_K), (0, 0))),
        vecs,
        conv_pw.astype(BF16),
        pool_bd.astype(BF16),
        w_out.astype(BF16),
        final_norm_w.reshape(1, D_MODEL),
    )

    n_seq_ctx = max(d for d in range(1, CTX_SEQS_PER_STEP + 1) if n_ctx % d == 0)
    y_prompt, new_k, new_v = _mixer_layers(x_prompt, mods, lambda s: n_lat, weights, n_seq_ctx)
    new_cache_k = new_k.reshape(n_ctx, DEPTH, ctx_len, N_KV_HEADS, HEAD_DIM)
    new_cache_v = new_v.reshape(n_ctx, DEPTH, ctx_len, N_KV_HEADS, HEAD_DIM)

    (y_sample,) = _mixer_layers(
        x_sample, mods, lambda s: s, weights, 1,
        rope=_rope_tables(lat_len),
        cache=(cache_k.reshape(n_lat, DEPTH, past, KV_W), cache_v.reshape(n_lat, DEPTH, past, KV_W)))
    return (y_prompt, y_sample, new_cache_k, new_cache_v)
```

```python
import functools

import jax
import jax.numpy as jnp
from jax import lax
from jax.experimental import pallas as pl
from jax.experimental.pallas import tpu as pltpu

D_MODEL = 1024
DEPTH = 4
GRID_W = 64
CONV_W = 256
ATTN_W = 512
POOL_W = 256
HEAD_DIM = 64
N_HEADS = 8
N_KV_HEADS = 2
GQA_GROUP = 4
KV_W = 128
WINDOW = 128
BLOCK = 128
CONV_K = 31
POOL_WINDOWS = (2, 4, 8, 16)
POOL_GROUP_W = 64
ROPE_BASE = 10000.0
EPS = 1e-6
NEG_INF = -1e30
LOG2E = 1.4426950408889634

OFF_A = 0
OFF_Q = 3 * CONV_W
OFF_KV = OFF_Q + ATTN_W
OFF_B_GATE = OFF_KV + 2 * KV_W
OFF_C = OFF_B_GATE + ATTN_W
IN_W = OFF_C + 2 * POOL_W

LANES = 128
CONV_K_PADDED = 32
CONV_PAD = 16
POOL_PAD = 8
ROW_CHUNK = 256
CONV_TILE = 32
ATTN_UNROLL = 2
CTX_SEQS_PER_STEP = 4
MOD_ROWS = 16
VMEM_LIMIT_BYTES = 56 * 1024 * 1024

F32 = jnp.float32
BF16 = jnp.bfloat16


def _silu(x):
    return x * jax.nn.sigmoid(x)


def _lane_iota(shape):
    return lax.broadcasted_iota(jnp.int32, shape, 1)


def _dup_halves(x):
    r = pltpu.roll(x, HEAD_DIM, axis=1)
    lo = _lane_iota(x.shape) < HEAD_DIM
    return jnp.where(lo, x, r), jnp.where(lo, r, x)


def _with_ones(x):
    r = pltpu.roll(x, HEAD_DIM, axis=1)
    lo = _lane_iota(x.shape) < HEAD_DIM
    return jnp.where(lo, x, 1.0), jnp.where(lo, r, 1.0)


def _rope(x, cos, sin):
    quarter = HEAD_DIM // 4
    fwd = pltpu.roll(x, LANES - quarter, axis=1)
    bwd = pltpu.roll(x, quarter, axis=1)
    first = (_lane_iota(x.shape) & (2 * quarter - 1)) < quarter
    return x * cos + jnp.where(first, fwd, bwd) * sin


def _rms(x):
    return x * lax.rsqrt(jnp.mean(x * x, axis=-1, keepdims=True) + EPS)


def _adaln_kernel(c_ref, w_ref, b_ref, o_ref):
    s = _silu(c_ref[...])
    o_ref[0] = jnp.dot(s.astype(BF16), w_ref[0].astype(BF16), preferred_element_type=F32) + b_ref[0]


def _adaln(cvecs, w_ada, b_ada):
    n_col = 3
    return pl.pallas_call(
        _adaln_kernel,
        grid=(DEPTH, n_col),
        in_specs=[
            pl.BlockSpec((MOD_ROWS, D_MODEL), lambda l, j: (0, 0)),
            pl.BlockSpec((1, D_MODEL, D_MODEL), lambda l, j: (l, 0, j)),
            pl.BlockSpec((1, 1, D_MODEL), lambda l, j: (l, 0, j)),
        ],
        out_specs=pl.BlockSpec((1, MOD_ROWS, D_MODEL), lambda l, j: (l, 0, j)),
        out_shape=jax.ShapeDtypeStruct((DEPTH, MOD_ROWS, 3 * D_MODEL), F32),
        compiler_params=pltpu.CompilerParams(
            dimension_semantics=("arbitrary", "arbitrary"),
            vmem_limit_bytes=VMEM_LIMIT_BYTES),
        name="adaln_mod",
    )(cvecs, w_ada, b_ada.reshape(DEPTH, 1, 3 * D_MODEL))


def _layers_kernel(*refs, L, n_seq, latent):
    if latent:
        (sink_ref, x_ref, mod_ref, nw_ref, win_ref, dw_ref, vec_ref, pw_ref, pbd_ref, wout_ref,
         fnw_ref, cos_ref, sin_ref, ck_ref, cv_ref,
         o_ref,
         gpad_ref, cpad_ref, gate_ref, qm_ref, kk_ref, vvt_ref, mix_ref, st_ref, pt_ref, es_ref,
         ckk_ref, cvvt_ref) = refs
    else:
        (sink_ref, x_ref, mod_ref, nw_ref, win_ref, dw_ref, vec_ref, pw_ref, pbd_ref, wout_ref,
         fnw_ref,
         o_ref, nk_ref, nv_ref,
         gpad_ref, cpad_ref, gate_ref, qm_ref, kk_ref, vvt_ref, mix_ref, st_ref, pt_ref, es_ref) = refs

    l = pl.program_id(1)
    n_chunks = n_seq * L // ROW_CHUNK
    kpad = BLOCK if latent else 0
    assert not latent or n_seq == 1

    def locate(r0):
        return divmod(r0, L)

    @pl.when(l == 0)
    def _():
        o_ref[...] = x_ref[...]

    for c in range(CONV_W // LANES):
        for s in range(n_seq):
            gpad_ref[c, s, 0:CONV_PAD, :] = jnp.zeros((CONV_PAD, LANES), F32)
            gpad_ref[c, s, CONV_PAD + L:CONV_PAD + L + CONV_PAD, :] = jnp.zeros((CONV_PAD, LANES), F32)
            cpad_ref[c, s, 0:POOL_PAD, :] = jnp.zeros((POOL_PAD, LANES), F32)
            cpad_ref[c, s, POOL_PAD + L:POOL_PAD + L + POOL_PAD, :] = jnp.zeros((POOL_PAD, LANES), F32)
    if latent:
        for g in range(N_KV_HEADS):
            kk_ref[g, 0:kpad, :] = jnp.zeros((kpad, LANES), BF16)
            kk_ref[g, kpad + L:kpad + L + kpad, :] = jnp.zeros((kpad, LANES), BF16)
            vvt_ref[g, :, 0:kpad] = jnp.zeros((LANES, kpad), BF16)
            vvt_ref[g, :, kpad + L:kpad + L + kpad] = jnp.zeros((LANES, kpad), BF16)
        ck0, ck1 = _dup_halves(ck_ref[0, 0])
        cv0, cv1 = _with_ones(cv_ref[0, 0])
        ckk_ref[0] = ck0.astype(BF16)
        ckk_ref[1] = ck1.astype(BF16)
        cvvt_ref[0] = cv0.T.astype(BF16)
        cvvt_ref[1] = cv1.T.astype(BF16)

    shift = mod_ref[0, 0, 0:1, :]
    scl = mod_ref[0, 0, 1:2, :]
    gate = mod_ref[0, 0, 2:3, :]
    norm_scale = nw_ref[0] * (1.0 + scl)

    def proj_chunk(c):
        r0 = c * ROW_CHUNK
        seq, rs = locate(r0)
        rows = pl.ds(r0, ROW_CHUNK)
        seq_rows = pl.ds(rs, ROW_CHUNK)
        x = o_ref[seq, seq_rows, :]
        h = _rms(x) * norm_scale + shift
        hb = h.astype(BF16)

        ua = jnp.dot(hb, win_ref[0, :, OFF_A:OFF_Q], preferred_element_type=F32)
        glu = ua[:, 0:CONV_W] * jax.nn.sigmoid(ua[:, CONV_W:2 * CONV_W])
        for c in range(CONV_W // LANES):
            gpad_ref[c, seq, pl.ds(rs + CONV_PAD, ROW_CHUNK), :] = glu[:, c * LANES:(c + 1) * LANES]
        gate_ref[rows, 0:CONV_W] = _silu(ua[:, 2 * CONV_W:3 * CONV_W])

        uq = jnp.dot(hb, win_ref[0, :, OFF_Q:OFF_KV], preferred_element_type=F32)
        ukv = jnp.dot(hb, win_ref[0, :, OFF_KV:OFF_B_GATE], preferred_element_type=F32)
        k = ukv[:, 0:KV_W]
        v = ukv[:, KV_W:2 * KV_W]
        if latent:
            cos = cos_ref[seq_rows, :]
            sin = sin_ref[seq_rows, :]
            k = _rope(k, cos, sin)
        else:
            nk_ref[seq, 0, seq_rows, :] = k
            nv_ref[seq, 0, seq_rows, :] = v
        lo = _lane_iota((ROW_CHUNK, LANES)) < HEAD_DIM
        scale = HEAD_DIM ** -0.5 * LOG2E
        for p in range(ATTN_W // LANES):
            qp = uq[:, p * LANES:(p + 1) * LANES]
            if latent:
                qp = _rope(qp, cos, sin)
            qp = qp * scale
            g, c = divmod(2 * p, GQA_GROUP)
            qm_ref[g, rows, c * LANES:(c + 1) * LANES] = jnp.where(lo, qp, 0.0).astype(BF16)
            qm_ref[g, rows, (c + 1) * LANES:(c + 2) * LANES] = jnp.where(lo, 0.0, qp).astype(BF16)
        k0, k1 = _dup_halves(k)
        v0, v1 = _with_ones(v)
        krows = pl.ds(r0 + kpad, ROW_CHUNK)
        kk_ref[0, krows, :] = k0.astype(BF16)
        kk_ref[1, krows, :] = k1.astype(BF16)
        vvt_ref[0, :, krows] = v0.T.astype(BF16)
        vvt_ref[1, :, krows] = v1.T.astype(BF16)

        ub = jnp.dot(hb, win_ref[0, :, OFF_B_GATE:OFF_C], preferred_element_type=F32)
        gate_ref[rows, CONV_W:CONV_W + ATTN_W] = _silu(ub)

        uc = jnp.dot(hb, win_ref[0, :, OFF_C:IN_W], preferred_element_type=F32)
        for c in range(POOL_W // LANES):
            cpad_ref[c, seq, pl.ds(rs + POOL_PAD, ROW_CHUNK), :] = uc[:, c * LANES:(c + 1) * LANES]
        gate_ref[rows, CONV_W + ATTN_W:D_MODEL] = _silu(uc[:, POOL_W:2 * POOL_W])

    conv_b = vec_ref[0, 0:1, :]
    ln_g = vec_ref[0, 1:2, :]
    ln_b = vec_ref[0, 2:3, :]
    pool_scale = vec_ref[0, 3:4, :]

    def conv_rows(r0):
        seq, rs = locate(r0)
        rows = pl.ds(r0, BLOCK)
        pieces = []
        for t in range(BLOCK // CONV_TILE):
            base = rs + t * CONV_TILE + (CONV_PAD - CONV_K // 2)
            halves = []
            for c in range(CONV_W // LANES):
                acc = jnp.zeros((CONV_TILE, LANES), F32)
                for kk in range(CONV_K):
                    acc = acc + (gpad_ref[c, seq, pl.ds(base + kk, CONV_TILE), :]
                                 * dw_ref[0, kk:kk + 1, c * LANES:(c + 1) * LANES])
                halves.append(acc)
            y = jnp.concatenate(halves, axis=1) + conv_b
            mu = jnp.mean(y, axis=-1, keepdims=True)
            yc = y - mu
            var = jnp.mean(yc * yc, axis=-1, keepdims=True)
            yn = (yc * lax.rsqrt(var + EPS)) * ln_g + ln_b
            pieces.append(_silu(yn).astype(BF16))
        z = jnp.concatenate(pieces, axis=0)
        a = jnp.dot(z, pw_ref[0], preferred_element_type=F32)
        mix_ref[rows, 0:CONV_W] = (a * gate_ref[rows, 0:CONV_W]).astype(BF16)

    def pool_rows(r0):
        seq, rs = locate(r0)
        rows = pl.ds(r0, BLOCK)
        t = rs + lax.broadcasted_iota(jnp.int32, (BLOCK, LANES), 0)
        first_group = _lane_iota((BLOCK, LANES)) < POOL_GROUP_W
        deltas = []
        for c in range(POOL_W // LANES):

            def tok(off, c=c):
                return cpad_ref[c, seq, pl.ds(rs + POOL_PAD + off, BLOCK), :]

            centre = tok(0)
            half_a, half_b = POOL_WINDOWS[2 * c] // 2, POOL_WINDOWS[2 * c + 1] // 2
            sum_a = centre
            for off in list(range(-half_a, 0)) + list(range(1, half_a)):
                sum_a = sum_a + tok(off)
            sum_b = sum_a
            for off in list(range(-half_b, -half_a)) + list(range(half_a, half_b)):
                sum_b = sum_b + tok(off)
            half_w = jnp.where(first_group, half_a, half_b)
            cnt = jnp.minimum(t + half_w, L) - jnp.maximum(t - half_w, 0)
            mean = jnp.where(first_group, sum_a, sum_b) / cnt.astype(F32)
            deltas.append((mean - centre).astype(BF16))
        d = jnp.concatenate(deltas, axis=1)
        c_out = jnp.dot(d, pbd_ref[0], preferred_element_type=F32) * pool_scale
        c0 = CONV_W + ATTN_W
        mix_ref[rows, c0:D_MODEL] = (c_out * gate_ref[rows, c0:D_MODEL]).astype(BF16)

    n_blocks = n_seq * L // BLOCK
    seq_blocks = L // BLOCK
    n_local = 3 * BLOCK if latent else L
    contract_last = (((1,), (1,)), ((), ()))

    def block_rows(i):
        if isinstance(i, int):
            q0, k0 = i * BLOCK, (i // seq_blocks) * L
        else:
            q0 = pl.multiple_of(i * BLOCK, BLOCK)
            k0 = pl.multiple_of(lax.div(i, jnp.int32(seq_blocks)) * L, L)
        return q0, (pl.ds(q0, n_local) if latent else pl.ds(k0, n_local))

    def scores(i, g):
        q0, local_keys = block_rows(i)
        qm = jnp.concatenate([qm_ref[g, pl.ds(q0, BLOCK), j * LANES:(j + 1) * LANES]
                              for j in range(GQA_GROUP)], axis=0)
        st_ref[g, 0:n_local, :] = lax.dot_general(kk_ref[g, local_keys, :], qm, contract_last,
                                                  preferred_element_type=F32)
        if latent:
            st_ref[g, n_local:, :] = lax.dot_general(ckk_ref[g], qm, contract_last,
                                                     preferred_element_type=F32)

    def softmax_pv(i, g):
        q0, local_keys = block_rows(i)
        qrows = pl.ds(q0, BLOCK)
        if latent:
            jk = lax.broadcasted_iota(jnp.int32, (BLOCK, BLOCK), 0)
            jq = lax.broadcasted_iota(jnp.int32, (BLOCK, BLOCK), 1)
            prev_ok = jk >= jq + jnp.where(i > 0, 0, BLOCK)
            next_ok = jk <= jq - jnp.where(i < n_blocks - 1, 0, BLOCK)
            key_rows = [(0, BLOCK), (BLOCK, 2 * BLOCK), (2 * BLOCK, 3 * BLOCK), (n_local, st_ref.shape[1])]
        else:
            key_rows = [(0, n_local)]
        for j in range(GQA_GROUP):
            cols = slice(j * BLOCK, (j + 1) * BLOCK)
            sink = sink_ref[l, g * GQA_GROUP + j] * LOG2E
            pieces = [st_ref[g, a:b, cols] for a, b in key_rows]
            if latent:
                pieces[0] = jnp.where(prev_ok, pieces[0], NEG_INF)
                pieces[2] = jnp.where(next_ok, pieces[2], NEG_INF)
            m = jnp.full((1, BLOCK), sink, F32)
            for piece in pieces:
                m = jnp.maximum(m, jnp.max(piece, axis=0, keepdims=True))
            for (a, b), piece in zip(key_rows, pieces):
                pt_ref[g, a:b, cols] = jnp.exp2(piece - m).astype(BF16)
            es_ref[g, :, cols] = jnp.broadcast_to(jnp.exp2(sink - m), (8, BLOCK))
        ot = jnp.dot(vvt_ref[g, :, local_keys], pt_ref[g, 0:n_local, :],
                     preferred_element_type=F32)
        if latent:
            ot = ot + jnp.dot(cvvt_ref[g], pt_ref[g, n_local:, :], preferred_element_type=F32)
        inv = 1.0 / (ot[HEAD_DIM:HEAD_DIM + 8, :] + es_ref[g])
        on = ot[0:HEAD_DIM, :] * inv[0:1, :]
        pairs = []
        for p in range(GQA_GROUP // 2):
            two_heads = jnp.concatenate([on[:, (2 * p) * BLOCK:(2 * p + 1) * BLOCK],
                                         on[:, (2 * p + 1) * BLOCK:(2 * p + 2) * BLOCK]], axis=0)
            pairs.append(two_heads.T)
        c0 = CONV_W + g * (ATTN_W // N_KV_HEADS)
        cols = slice(c0, c0 + ATTN_W // N_KV_HEADS)
        mix_ref[qrows, cols] = (jnp.concatenate(pairs, axis=1) * gate_ref[qrows, cols]).astype(BF16)

    def attn_block(i, carry):
        scores(i, 1)
        softmax_pv(i, 0)
        scores(jnp.minimum(i + 1, n_blocks - 1), 0)
        softmax_pv(i, 1)
        return carry

    def out_chunk(c):
        seq, rs = locate(c * ROW_CHUNK)
        seq_rows = pl.ds(rs, ROW_CHUNK)
        y = jnp.dot(mix_ref[pl.ds(c * ROW_CHUNK, ROW_CHUNK), :], wout_ref[0], preferred_element_type=F32)
        o_ref[seq, seq_rows, :] = o_ref[seq, seq_rows, :] + gate * y

    def vpu_branches(c):
        for b in range(ROW_CHUNK // BLOCK):
            conv_rows(c * ROW_CHUNK + b * BLOCK)
            pool_rows(c * ROW_CHUNK + b * BLOCK)

    for c in range(n_chunks):
        proj_chunk(c)
        if c > 0:
            vpu_branches(c - 1)
    scores(0, 0)
    lax.fori_loop(0, n_blocks, attn_block, 0, unroll=min(n_blocks, ATTN_UNROLL))
    last_blocks = [(n_chunks - 1) * ROW_CHUNK + b * BLOCK for b in range(ROW_CHUNK // BLOCK)]
    for c in range(n_chunks - 1):
        out_chunk(c)
        if last_blocks:
            r0 = last_blocks.pop(0)
            conv_rows(r0)
            pool_rows(r0)
    for r0 in last_blocks:
        conv_rows(r0)
        pool_rows(r0)
    out_chunk(n_chunks - 1)

    @pl.when(l == DEPTH - 1)
    def _():
        for c in range(n_chunks):
            seq, rs = locate(c * ROW_CHUNK)
            seq_rows = pl.ds(rs, ROW_CHUNK)
            o_ref[seq, seq_rows, :] = _rms(o_ref[seq, seq_rows, :]) * fnw_ref[...]


def _mixer_layers(x, mods, mod_row, weights, n_seq, rope=None, cache=None):
    S, L, _ = x.shape
    latent = cache is not None
    sink, norm_w, w_in, conv_dw, vecs, conv_pw, pool_bd, w_out, final_norm_w = weights
    kpad = BLOCK if latent else 0
    n_keys = 3 * BLOCK + cache[0].shape[2] if latent else L
    R = n_seq * L
    assert S % n_seq == 0 and R % ROW_CHUNK == 0 and L % BLOCK == 0 and (L % ROW_CHUNK == 0 or ROW_CHUNK % L == 0)

    per_layer = lambda *tail: (lambda s, l: (l,) + tail)
    in_specs = [
        pl.BlockSpec(memory_space=pltpu.SMEM),
        pl.BlockSpec((n_seq, L, D_MODEL), lambda s, l: (s, 0, 0)),
        pl.BlockSpec((1, 1, 3, D_MODEL), lambda s, l: (l, mod_row(s), 0, 0)),
        pl.BlockSpec((1, 1, D_MODEL), per_layer(0, 0)),
        pl.BlockSpec((1, D_MODEL, IN_W), per_layer(0, 0)),
        pl.BlockSpec((1, CONV_K_PADDED, CONV_W), per_layer(0, 0)),
        pl.BlockSpec((1, 8, CONV_W), per_layer(0, 0)),
        pl.BlockSpec((1, CONV_W, CONV_W), per_layer(0, 0)),
        pl.BlockSpec((1, POOL_W, POOL_W), per_layer(0, 0)),
        pl.BlockSpec((1, D_MODEL, D_MODEL), per_layer(0, 0)),
        pl.BlockSpec((1, D_MODEL), lambda s, l: (0, 0)),
    ]
    args = [sink, x, mods, norm_w, w_in, conv_dw, vecs, conv_pw, pool_bd, w_out, final_norm_w]
    out_specs = [pl.BlockSpec((n_seq, L, D_MODEL), lambda s, l: (s, 0, 0))]
    out_shape = [jax.ShapeDtypeStruct((S, L, D_MODEL), F32)]
    scratch = [
        pltpu.VMEM((CONV_W // LANES, n_seq, L + 2 * CONV_PAD, LANES), F32),
        pltpu.VMEM((POOL_W // LANES, n_seq, L + 2 * POOL_PAD, LANES), F32),
        pltpu.VMEM((R, D_MODEL), F32),
        pltpu.VMEM((N_KV_HEADS, R, ATTN_W), BF16),
        pltpu.VMEM((N_KV_HEADS, R + 2 * kpad, LANES), BF16),
        pltpu.VMEM((N_KV_HEADS, LANES, R + 2 * kpad), BF16),
        pltpu.VMEM((R, D_MODEL), BF16),
        pltpu.VMEM((N_KV_HEADS, n_keys, GQA_GROUP * BLOCK), F32),
        pltpu.VMEM((N_KV_HEADS, n_keys, GQA_GROUP * BLOCK), BF16),
        pltpu.VMEM((N_KV_HEADS, 8, GQA_GROUP * BLOCK), F32),
    ]
    if latent:
        cos, sin = rope
        ck, cv = cache
        past = ck.shape[2]
        in_specs += [
            pl.BlockSpec((L, LANES), lambda s, l: (0, 0)),
            pl.BlockSpec((L, LANES), lambda s, l: (0, 0)),
            pl.BlockSpec((1, 1, past, KV_W), lambda s, l: (s, l, 0, 0)),
            pl.BlockSpec((1, 1, past, KV_W), lambda s, l: (s, l, 0, 0)),
        ]
        args += [cos, sin, ck, cv]
        scratch += [pltpu.VMEM((N_KV_HEADS, past, LANES), BF16),
                    pltpu.VMEM((N_KV_HEADS, LANES, past), BF16)]
    else:
        out_specs += [pl.BlockSpec((n_seq, 1, L, KV_W), lambda s, l: (s, l, 0, 0))] * 2
        out_shape += [jax.ShapeDtypeStruct((S, DEPTH, L, KV_W), F32)] * 2

    return pl.pallas_call(
        functools.partial(_layers_kernel, L=L, n_seq=n_seq, latent=latent),
        grid=(S // n_seq, DEPTH),
        in_specs=in_specs,
        out_specs=out_specs,
        out_shape=out_shape,
        scratch_shapes=scratch,
        compiler_params=pltpu.CompilerParams(
            dimension_semantics=("arbitrary", "arbitrary"),
            vmem_limit_bytes=VMEM_LIMIT_BYTES),
        name="latent_layers" if latent else "context_layers",
    )(*args)


def _rope_tables(L):
    quarter = HEAD_DIM // 4
    t = jnp.arange(L)
    rows = (t // GRID_W).astype(F32)
    cols = (t % GRID_W).astype(F32)
    freqs = ROPE_BASE ** (-jnp.arange(quarter, dtype=F32) / quarter)
    ang_r = rows[:, None] * freqs[None, :]
    ang_c = cols[:, None] * freqs[None, :]
    cos = jnp.concatenate([jnp.cos(ang_r)] * 2 + [jnp.cos(ang_c)] * 2, axis=-1)
    sin = jnp.concatenate([-jnp.sin(ang_r), jnp.sin(ang_r), -jnp.sin(ang_c), jnp.sin(ang_c)], axis=-1)
    return jnp.tile(cos, (1, 2)), jnp.tile(sin, (1, 2))


def kernel(x_prompt, x_sample, c, cache_k, cache_v, c_ctx, w_ada, b_ada, norm_w, w_in, conv_dw, conv_b,
           conv_ln_g, conv_ln_b, conv_pw, attn_sink, pool_w, pool_scale, w_out, final_norm_w):
    n_ctx, ctx_len, _ = x_prompt.shape
    n_lat, lat_len, _ = x_sample.shape
    past = cache_k.shape[2]
    assert n_lat + 1 <= MOD_ROWS

    cvecs = jnp.zeros((MOD_ROWS, D_MODEL), F32).at[:n_lat].set(c).at[n_lat].set(c_ctx)
    mods = _adaln(cvecs, w_ada, b_ada).reshape(DEPTH, MOD_ROWS, 3, D_MODEL)

    n_groups = len(POOL_WINDOWS)
    eye = jnp.eye(n_groups, dtype=F32)
    pool_bd = (pool_w[:, :, :, None, :] * eye[None, :, None, :, None]).reshape(DEPTH, POOL_W, POOL_W)
    vecs = jnp.zeros((DEPTH, 8, CONV_W), F32)
    vecs = vecs.at[:, 0].set(conv_b).at[:, 1].set(conv_ln_g).at[:, 2].set(conv_ln_b).at[:, 3].set(pool_scale)
    weights = (
        attn_sink,
        norm_w.reshape(DEPTH, 1, D_MODEL),
        w_in.astype(BF16),
        jnp.pad(conv_dw, ((0, 0), (0, CONV_K_PADDED - CONV_K), (0, 0))),
        vecs,
        conv_pw.astype(BF16),
        pool_bd.astype(BF16),
        w_out.astype(BF16),
        final_norm_w.reshape(1, D_MODEL),
    )

    n_seq_ctx = max(d for d in range(1, CTX_SEQS_PER_STEP + 1) if n_ctx % d == 0)
    y_prompt, new_k, new_v = _mixer_layers(x_prompt, mods, lambda s: n_lat, weights, n_seq_ctx)
    new_cache_k = new_k.reshape(n_ctx, DEPTH, ctx_len, N_KV_HEADS, HEAD_DIM)
    new_cache_v = new_v.reshape(n_ctx, DEPTH, ctx_len, N_KV_HEADS, HEAD_DIM)

    (y_sample,) = _mixer_layers(
        x_sample, mods, lambda s: s, weights, 1,
        rope=_rope_tables(lat_len),
        cache=(cache_k.reshape(n_lat, DEPTH, past, KV_W), cache_v.reshape(n_lat, DEPTH, past, KV_W)))
    return (y_prompt, y_sample, new_cache_k, new_cache_v)
```

```python
import functools

import jax
import jax.numpy as jnp
from jax import lax
from jax.experimental import pallas as pl
from jax.experimental.pallas import tpu as pltpu

D_MODEL = 1024
DEPTH = 4
GRID_W = 64
CONV_W = 256
ATTN_W = 512
POOL_W = 256
HEAD_DIM = 64
N_HEADS = 8
N_KV_HEADS = 2
GQA_GROUP = 4
KV_W = 128
WINDOW = 128
BLOCK = 128
CONV_K = 31
POOL_WINDOWS = (2, 4, 8, 16)
POOL_GROUP_W = 64
ROPE_BASE = 10000.0
EPS = 1e-6
NEG_INF = -1e30
LOG2E = 1.4426950408889634

OFF_A = 0
OFF_Q = 3 * CONV_W
OFF_KV = OFF_Q + ATTN_W
OFF_B_GATE = OFF_KV + 2 * KV_W
OFF_C = OFF_B_GATE + ATTN_W
IN_W = OFF_C + 2 * POOL_W

LANES = 128
CONV_K_PADDED = 32
CONV_PAD = 16
POOL_PAD = 8
MAX_ROW_CHUNK = 512
CONV_TILE = 32
ATTN_UNROLL = 2
CTX_SEQS_PER_STEP = 4
ROW_SUMS_ON_MXU = {True: True, False: False}
SOFTMAX_ROWS = 32
MOD_ROWS = 16
VMEM_LIMIT_BYTES = 56 * 1024 * 1024

F32 = jnp.float32
BF16 = jnp.bfloat16


def _silu(x):
    return x * jax.nn.sigmoid(x)


def _lane_iota(shape):
    return lax.broadcasted_iota(jnp.int32, shape, 1)


def _dup_halves(x):
    r = pltpu.roll(x, HEAD_DIM, axis=1)
    lo = _lane_iota(x.shape) < HEAD_DIM
    return jnp.where(lo, x, r), jnp.where(lo, r, x)


def _with_ones(x):
    r = pltpu.roll(x, HEAD_DIM, axis=1)
    lo = _lane_iota(x.shape) < HEAD_DIM
    return jnp.where(lo, x, 1.0), jnp.where(lo, r, 1.0)


def _rope(x, cos, sin):
    quarter = HEAD_DIM // 4
    fwd = pltpu.roll(x, LANES - quarter, axis=1)
    bwd = pltpu.roll(x, quarter, axis=1)
    first = (_lane_iota(x.shape) & (2 * quarter - 1)) < quarter
    return x * cos + jnp.where(first, fwd, bwd) * sin


def _rms(x):
    return x * lax.rsqrt(jnp.mean(x * x, axis=-1, keepdims=True) + EPS)


def _adaln_kernel(c_ref, w_ref, b_ref, o_ref):
    s = _silu(c_ref[...])
    o_ref[0] = jnp.dot(s.astype(BF16), w_ref[0].astype(BF16), preferred_element_type=F32) + b_ref[0]


def _adaln(cvecs, w_ada, b_ada):
    n_col = 3
    return pl.pallas_call(
        _adaln_kernel,
        grid=(DEPTH, n_col),
        in_specs=[
            pl.BlockSpec((MOD_ROWS, D_MODEL), lambda l, j: (0, 0)),
            pl.BlockSpec((1, D_MODEL, D_MODEL), lambda l, j: (l, 0, j)),
            pl.BlockSpec((1, 1, D_MODEL), lambda l, j: (l, 0, j)),
        ],
        out_specs=pl.BlockSpec((1, MOD_ROWS, D_MODEL), lambda l, j: (l, 0, j)),
        out_shape=jax.ShapeDtypeStruct((DEPTH, MOD_ROWS, 3 * D_MODEL), F32),
        compiler_params=pltpu.CompilerParams(
            dimension_semantics=("arbitrary", "arbitrary"),
            vmem_limit_bytes=VMEM_LIMIT_BYTES),
        name="adaln_mod",
    )(cvecs, w_ada, b_ada.reshape(DEPTH, 1, 3 * D_MODEL))


def _layers_kernel(*refs, L, n_seq, row_chunk, latent):
    if latent:
        (sink_ref, x_ref, mod_ref, nw_ref, win_ref, dw_ref, vec_ref, pw_ref, pbd_ref, wout_ref,
         fnw_ref, cos_ref, sin_ref, ck_ref, cv_ref,
         o_ref,
         gpad_ref, cpad_ref, gate_ref, qm_ref, kk_ref, vv_ref, mix_ref, s_ref, p_ref, es_ref,
         ckk_ref, cvv_ref) = refs
    else:
        (sink_ref, x_ref, mod_ref, nw_ref, win_ref, dw_ref, vec_ref, pw_ref, pbd_ref, wout_ref,
         fnw_ref,
         o_ref, nk_ref, nv_ref,
         gpad_ref, cpad_ref, gate_ref, qm_ref, kk_ref, vv_ref, mix_ref, s_ref, p_ref, es_ref) = refs

    l = pl.program_id(1)
    n_chunks = n_seq * L // row_chunk
    kpad = BLOCK if latent else 0
    assert not latent or n_seq == 1

    def locate(r0):
        return divmod(r0, L)

    @pl.when(l == 0)
    def _():
        o_ref[...] = x_ref[...]

    for c in range(CONV_W // LANES):
        for s in range(n_seq):
            gpad_ref[c, s, 0:CONV_PAD, :] = jnp.zeros((CONV_PAD, LANES), F32)
            gpad_ref[c, s, CONV_PAD + L:CONV_PAD + L + CONV_PAD, :] = jnp.zeros((CONV_PAD, LANES), F32)
            cpad_ref[c, s, 0:POOL_PAD, :] = jnp.zeros((POOL_PAD, LANES), F32)
            cpad_ref[c, s, POOL_PAD + L:POOL_PAD + L + POOL_PAD, :] = jnp.zeros((POOL_PAD, LANES), F32)
    if latent:
        for g in range(N_KV_HEADS):
            for ref in (kk_ref, vv_ref):
                ref[g, 0:kpad, :] = jnp.zeros((kpad, LANES), BF16)
                ref[g, kpad + L:kpad + L + kpad, :] = jnp.zeros((kpad, LANES), BF16)
        ck0, ck1 = _dup_halves(ck_ref[0, 0])
        cv0, cv1 = _with_ones(cv_ref[0, 0]) if ROW_SUMS_ON_MXU[latent] else _dup_halves(cv_ref[0, 0])
        ckk_ref[0] = ck0.astype(BF16)
        ckk_ref[1] = ck1.astype(BF16)
        cvv_ref[0] = cv0.astype(BF16)
        cvv_ref[1] = cv1.astype(BF16)

    shift = mod_ref[0, 0, 0:1, :]
    scl = mod_ref[0, 0, 1:2, :]
    gate = mod_ref[0, 0, 2:3, :]
    norm_scale = nw_ref[0] * (1.0 + scl)

    def proj_chunk(c):
        r0 = c * row_chunk
        seq, rs = locate(r0)
        rows = pl.ds(r0, row_chunk)
        seq_rows = pl.ds(rs, row_chunk)
        x = o_ref[seq, seq_rows, :]
        h = _rms(x) * norm_scale + shift
        hb = h.astype(BF16)

        ua = jnp.dot(hb, win_ref[0, :, OFF_A:OFF_Q], preferred_element_type=F32)
        glu = ua[:, 0:CONV_W] * jax.nn.sigmoid(ua[:, CONV_W:2 * CONV_W])
        for c in range(CONV_W // LANES):
            gpad_ref[c, seq, pl.ds(rs + CONV_PAD, row_chunk), :] = glu[:, c * LANES:(c + 1) * LANES]
        gate_ref[rows, 0:CONV_W] = _silu(ua[:, 2 * CONV_W:3 * CONV_W])

        uq = jnp.dot(hb, win_ref[0, :, OFF_Q:OFF_KV], preferred_element_type=F32)
        ukv = jnp.dot(hb, win_ref[0, :, OFF_KV:OFF_B_GATE], preferred_element_type=F32)
        k = ukv[:, 0:KV_W]
        v = ukv[:, KV_W:2 * KV_W]
        if latent:
            cos = cos_ref[seq_rows, :]
            sin = sin_ref[seq_rows, :]
            k = _rope(k, cos, sin)
        else:
            nk_ref[seq, 0, seq_rows, :] = k
            nv_ref[seq, 0, seq_rows, :] = v
        lo = _lane_iota((row_chunk, LANES)) < HEAD_DIM
        scale = HEAD_DIM ** -0.5 * LOG2E
        for p in range(ATTN_W // LANES):
            qp = uq[:, p * LANES:(p + 1) * LANES]
            if latent:
                qp = _rope(qp, cos, sin)
            qp = qp * scale
            g, c = divmod(2 * p, GQA_GROUP)
            qm_ref[g, rows, c * LANES:(c + 1) * LANES] = jnp.where(lo, qp, 0.0).astype(BF16)
            qm_ref[g, rows, (c + 1) * LANES:(c + 2) * LANES] = jnp.where(lo, 0.0, qp).astype(BF16)
        k0, k1 = _dup_halves(k)
        v0, v1 = _with_ones(v) if ROW_SUMS_ON_MXU[latent] else _dup_halves(v)
        krows = pl.ds(r0 + kpad, row_chunk)
        kk_ref[0, krows, :] = k0.astype(BF16)
        kk_ref[1, krows, :] = k1.astype(BF16)
        vv_ref[0, krows, :] = v0.astype(BF16)
        vv_ref[1, krows, :] = v1.astype(BF16)

        ub = jnp.dot(hb, win_ref[0, :, OFF_B_GATE:OFF_C], preferred_element_type=F32)
        gate_ref[rows, CONV_W:CONV_W + ATTN_W] = _silu(ub)

        uc = jnp.dot(hb, win_ref[0, :, OFF_C:IN_W], preferred_element_type=F32)
        for c in range(POOL_W // LANES):
            cpad_ref[c, seq, pl.ds(rs + POOL_PAD, row_chunk), :] = uc[:, c * LANES:(c + 1) * LANES]
        gate_ref[rows, CONV_W + ATTN_W:D_MODEL] = _silu(uc[:, POOL_W:2 * POOL_W])

    conv_b = vec_ref[0, 0:1, :]
    ln_g = vec_ref[0, 1:2, :]
    ln_b = vec_ref[0, 2:3, :]
    pool_scale = vec_ref[0, 3:4, :]

    def conv_rows(r0):
        seq, rs = locate(r0)
        rows = pl.ds(r0, BLOCK)
        pieces = []
        for t in range(BLOCK // CONV_TILE):
            base = rs + t * CONV_TILE + (CONV_PAD - CONV_K // 2)
            halves = []
            for c in range(CONV_W // LANES):
                acc = jnp.zeros((CONV_TILE, LANES), F32)
                for kk in range(CONV_K):
                    acc = acc + (gpad_ref[c, seq, pl.ds(base + kk, CONV_TILE), :]
                                 * dw_ref[0, kk:kk + 1, c * LANES:(c + 1) * LANES])
                halves.append(acc)
            y = jnp.concatenate(halves, axis=1) + conv_b
            mu = jnp.mean(y, axis=-1, keepdims=True)
            yc = y - mu
            var = jnp.mean(yc * yc, axis=-1, keepdims=True)
            yn = (yc * lax.rsqrt(var + EPS)) * ln_g + ln_b
            pieces.append(_silu(yn).astype(BF16))
        z = jnp.concatenate(pieces, axis=0)
        a = jnp.dot(z, pw_ref[0], preferred_element_type=F32)
        mix_ref[rows, 0:CONV_W] = (a * gate_ref[rows, 0:CONV_W]).astype(BF16)

    def pool_rows(r0):
        seq, rs = locate(r0)
        rows = pl.ds(r0, BLOCK)
        t = rs + lax.broadcasted_iota(jnp.int32, (BLOCK, LANES), 0)
        first_group = _lane_iota((BLOCK, LANES)) < POOL_GROUP_W
        deltas = []
        for c in range(POOL_W // LANES):

            def tok(off, c=c):
                return cpad_ref[c, seq, pl.ds(rs + POOL_PAD + off, BLOCK), :]

            centre = tok(0)
            half_a, half_b = POOL_WINDOWS[2 * c] // 2, POOL_WINDOWS[2 * c + 1] // 2
            sum_a = centre
            for off in list(range(-half_a, 0)) + list(range(1, half_a)):
                sum_a = sum_a + tok(off)
            sum_b = sum_a
            for off in list(range(-half_b, -half_a)) + list(range(half_a, half_b)):
                sum_b = sum_b + tok(off)
            half_w = jnp.where(first_group, half_a, half_b)
            cnt = jnp.minimum(t + half_w, L) - jnp.maximum(t - half_w, 0)
            mean = jnp.where(first_group, sum_a, sum_b) / cnt.astype(F32)
            deltas.append((mean - centre).astype(BF16))
        d = jnp.concatenate(deltas, axis=1)
        c_out = jnp.dot(d, pbd_ref[0], preferred_element_type=F32) * pool_scale
        c0 = CONV_W + ATTN_W
        mix_ref[rows, c0:D_MODEL] = (c_out * gate_ref[rows, c0:D_MODEL]).astype(BF16)

    n_blocks = n_seq * L // BLOCK
    seq_blocks = L // BLOCK
    n_local = 3 * BLOCK if latent else L
    lo_q = _lane_iota((BLOCK, LANES)) < HEAD_DIM
    contract_last = (((1,), (1,)), ((), ()))

    def fold(pieces, op):
        acc = None
        for piece in pieces:
            for c in range(piece.shape[1] // LANES):
                slab = piece[:, c * LANES:(c + 1) * LANES]
                acc = slab if acc is None else op(acc, slab)
        return acc

    def block_rows(i):
        if isinstance(i, int):
            q0, k0 = i * BLOCK, (i // seq_blocks) * L
        else:
            q0 = pl.multiple_of(i * BLOCK, BLOCK)
            k0 = pl.multiple_of(lax.div(i, jnp.int32(seq_blocks)) * L, L)
        return q0, (pl.ds(q0, n_local) if latent else pl.ds(k0, n_local))

    def scores(i, g):
        q0, local_rows = block_rows(i)
        qm = jnp.concatenate([qm_ref[g, pl.ds(q0, BLOCK), j * LANES:(j + 1) * LANES]
                              for j in range(GQA_GROUP)], axis=0)
        s_ref[g, :, 0:n_local] = lax.dot_general(qm, kk_ref[g, local_rows, :], contract_last,
                                                 preferred_element_type=F32)
        if latent:
            s_ref[g, :, n_local:] = lax.dot_general(qm, ckk_ref[g], contract_last,
                                                    preferred_element_type=F32)

    def softmax_pv(i, g):
        q0, local_rows = block_rows(i)
        qrows = pl.ds(q0, BLOCK)
        if latent:
            jq = lax.broadcasted_iota(jnp.int32, (SOFTMAX_ROWS, BLOCK), 0)
            jk = lax.broadcasted_iota(jnp.int32, (SOFTMAX_ROWS, BLOCK), 1)
            prev_off = jnp.where(i > 0, 0, BLOCK)
            next_off = jnp.where(i < n_blocks - 1, 0, BLOCK)
        for j in range(GQA_GROUP):
            sink = sink_ref[l, g * GQA_GROUP + j] * LOG2E
            for r in range(BLOCK // SOFTMAX_ROWS):
                rr = slice(j * BLOCK + r * SOFTMAX_ROWS, j * BLOCK + (r + 1) * SOFTMAX_ROWS)
                sj = s_ref[g, rr, :]
                if latent:
                    prev_ok = jk >= jq + (prev_off + r * SOFTMAX_ROWS)
                    next_ok = jk <= jq + (r * SOFTMAX_ROWS - next_off)
                    pieces = [jnp.where(prev_ok, sj[:, 0:BLOCK], NEG_INF),
                              sj[:, BLOCK:2 * BLOCK],
                              jnp.where(next_ok, sj[:, 2 * BLOCK:3 * BLOCK], NEG_INF),
                              sj[:, n_local:]]
                else:
                    pieces = [sj]
                m = jnp.maximum(jnp.max(fold(pieces, jnp.maximum), axis=-1, keepdims=True), sink)
                e = [jnp.exp2(piece - m) for piece in pieces]
                p_ref[g, rr, :] = jnp.concatenate(e, axis=1).astype(BF16)
                tail = jnp.exp2(sink - m)
                if not ROW_SUMS_ON_MXU[latent]:
                    tail = tail + jnp.sum(fold(e, jnp.add), axis=-1, keepdims=True)
                es_ref[g, rr, :] = jnp.broadcast_to(tail, (SOFTMAX_ROWS, LANES))
        o = jnp.dot(p_ref[g, :, 0:n_local], vv_ref[g, local_rows, :],
                    preferred_element_type=F32)
        if latent:
            o = o + jnp.dot(p_ref[g, :, n_local:], cvv_ref[g], preferred_element_type=F32)
        pairs = []
        for p in range(GQA_GROUP // 2):
            rows_a = slice((2 * p) * BLOCK, (2 * p + 1) * BLOCK)
            rows_b = slice((2 * p + 1) * BLOCK, (2 * p + 2) * BLOCK)
            o_a = o[rows_a]
            o_b = o[rows_b]
            if ROW_SUMS_ON_MXU[latent]:
                inv_a = 1.0 / (o_a + es_ref[g, rows_a, :])
                inv_b = 1.0 / (o_b + es_ref[g, rows_b, :])
                pairs.append(jnp.where(lo_q, o_a * pltpu.roll(inv_a, HEAD_DIM, axis=1),
                                       pltpu.roll(o_b, HEAD_DIM, axis=1) * inv_b))
            else:
                pairs.append(jnp.where(lo_q, o_a / es_ref[g, rows_a, :], o_b / es_ref[g, rows_b, :]))
        c0 = CONV_W + g * (ATTN_W // N_KV_HEADS)
        cols = slice(c0, c0 + ATTN_W // N_KV_HEADS)
        mix_ref[qrows, cols] = (jnp.concatenate(pairs, axis=1) * gate_ref[qrows, cols]).astype(BF16)

    def attn_block(i, carry):
        scores(i, 1)
        softmax_pv(i, 0)
        scores(jnp.minimum(i + 1, n_blocks - 1), 0)
        softmax_pv(i, 1)
        return carry

    def out_chunk(c):
        seq, rs = locate(c * row_chunk)
        seq_rows = pl.ds(rs, row_chunk)
        y = jnp.dot(mix_ref[pl.ds(c * row_chunk, row_chunk), :], wout_ref[0], preferred_element_type=F32)
        o_ref[seq, seq_rows, :] = o_ref[seq, seq_rows, :] + gate * y

    def vpu_branches(c):
        for b in range(row_chunk // BLOCK):
            conv_rows(c * row_chunk + b * BLOCK)
            pool_rows(c * row_chunk + b * BLOCK)

    for c in range(n_chunks):
        proj_chunk(c)
        if c > 0:
            vpu_branches(c - 1)
    scores(0, 0)
    lax.fori_loop(0, n_blocks, attn_block, 0, unroll=min(n_blocks, ATTN_UNROLL))
    last_blocks = [(n_chunks - 1) * row_chunk + b * BLOCK for b in range(row_chunk // BLOCK)]
    for c in range(n_chunks - 1):
        out_chunk(c)
        if last_blocks:
            r0 = last_blocks.pop(0)
            conv_rows(r0)
            pool_rows(r0)
    for r0 in last_blocks:
        conv_rows(r0)
        pool_rows(r0)
    out_chunk(n_chunks - 1)

    @pl.when(l == DEPTH - 1)
    def _():
        for c in range(n_chunks):
            seq, rs = locate(c * row_chunk)
            seq_rows = pl.ds(rs, row_chunk)
            o_ref[seq, seq_rows, :] = _rms(o_ref[seq, seq_rows, :]) * fnw_ref[...]


def _mixer_layers(x, mods, mod_row, weights, n_seq, rope=None, cache=None):
    S, L, _ = x.shape
    latent = cache is not None
    sink, norm_w, w_in, conv_dw, vecs, conv_pw, pool_bd, w_out, final_norm_w = weights
    kpad = BLOCK if latent else 0
    n_keys = 3 * BLOCK + cache[0].shape[2] if latent else L
    R = n_seq * L
    row_chunk = min(MAX_ROW_CHUNK, L)
    assert S % n_seq == 0 and L % row_chunk == 0 and row_chunk % BLOCK == 0

    per_layer = lambda *tail: (lambda s, l: (l,) + tail)
    in_specs = [
        pl.BlockSpec(memory_space=pltpu.SMEM),
        pl.BlockSpec((n_seq, L, D_MODEL), lambda s, l: (s, 0, 0)),
        pl.BlockSpec((1, 1, 3, D_MODEL), lambda s, l: (l, mod_row(s), 0, 0)),
        pl.BlockSpec((1, 1, D_MODEL), per_layer(0, 0)),
        pl.BlockSpec((1, D_MODEL, IN_W), per_layer(0, 0)),
        pl.BlockSpec((1, CONV_K_PADDED, CONV_W), per_layer(0, 0)),
        pl.BlockSpec((1, 8, CONV_W), per_layer(0, 0)),
        pl.BlockSpec((1, CONV_W, CONV_W), per_layer(0, 0)),
        pl.BlockSpec((1, POOL_W, POOL_W), per_layer(0, 0)),
        pl.BlockSpec((1, D_MODEL, D_MODEL), per_layer(0, 0)),
        pl.BlockSpec((1, D_MODEL), lambda s, l: (0, 0)),
    ]
    args = [sink, x, mods, norm_w, w_in, conv_dw, vecs, conv_pw, pool_bd, w_out, final_norm_w]
    out_specs = [pl.BlockSpec((n_seq, L, D_MODEL), lambda s, l: (s, 0, 0))]
    out_shape = [jax.ShapeDtypeStruct((S, L, D_MODEL), F32)]
    scratch = [
        pltpu.VMEM((CONV_W // LANES, n_seq, L + 2 * CONV_PAD, LANES), F32),
        pltpu.VMEM((POOL_W // LANES, n_seq, L + 2 * POOL_PAD, LANES), F32),
        pltpu.VMEM((R, D_MODEL), F32),
        pltpu.VMEM((N_KV_HEADS, R, ATTN_W), BF16),
        pltpu.VMEM((N_KV_HEADS, R + 2 * kpad, LANES), BF16),
        pltpu.VMEM((N_KV_HEADS, R + 2 * kpad, LANES), BF16),
        pltpu.VMEM((R, D_MODEL), BF16),
        pltpu.VMEM((N_KV_HEADS, GQA_GROUP * BLOCK, n_keys), F32),
        pltpu.VMEM((N_KV_HEADS, GQA_GROUP * BLOCK, n_keys), BF16),
        pltpu.VMEM((N_KV_HEADS, GQA_GROUP * BLOCK, LANES), F32),
    ]
    if latent:
        cos, sin = rope
        ck, cv = cache
        past = ck.shape[2]
        in_specs += [
            pl.BlockSpec((L, LANES), lambda s, l: (0, 0)),
            pl.BlockSpec((L, LANES), lambda s, l: (0, 0)),
            pl.BlockSpec((1, 1, past, KV_W), lambda s, l: (s, l, 0, 0)),
            pl.BlockSpec((1, 1, past, KV_W), lambda s, l: (s, l, 0, 0)),
        ]
        args += [cos, sin, ck, cv]
        scratch += [pltpu.VMEM((N_KV_HEADS, past, LANES), BF16),
                    pltpu.VMEM((N_KV_HEADS, past, LANES), BF16)]
    else:
        out_specs += [pl.BlockSpec((n_seq, 1, L, KV_W), lambda s, l: (s, l, 0, 0))] * 2
        out_shape += [jax.ShapeDtypeStruct((S, DEPTH, L, KV_W), F32)] * 2

    return pl.pallas_call(
        functools.partial(_layers_kernel, L=L, n_seq=n_seq, row_chunk=row_chunk, latent=latent),
        grid=(S // n_seq, DEPTH),
        in_specs=in_specs,
        out_specs=out_specs,
        out_shape=out_shape,
        scratch_shapes=scratch,
        compiler_params=pltpu.CompilerParams(
            dimension_semantics=("arbitrary", "arbitrary"),
            vmem_limit_bytes=VMEM_LIMIT_BYTES),
        name="latent_layers" if latent else "context_layers",
    )(*args)


def _rope_tables(L):
    quarter = HEAD_DIM // 4
    t = jnp.arange(L)
    rows = (t // GRID_W).astype(F32)
    cols = (t % GRID_W).astype(F32)
    freqs = ROPE_BASE ** (-jnp.arange(quarter, dtype=F32) / quarter)
    ang_r = rows[:, None] * freqs[None, :]
    ang_c = cols[:, None] * freqs[None, :]
    cos = jnp.concatenate([jnp.cos(ang_r)] * 2 + [jnp.cos(ang_c)] * 2, axis=-1)
    sin = jnp.concatenate([-jnp.sin(ang_r), jnp.sin(ang_r), -jnp.sin(ang_c), jnp.sin(ang_c)], axis=-1)
    return jnp.tile(cos, (1, 2)), jnp.tile(sin, (1, 2))


def kernel(x_prompt, x_sample, c, cache_k, cache_v, c_ctx, w_ada, b_ada, norm_w, w_in, conv_dw, conv_b,
           conv_ln_g, conv_ln_b, conv_pw, attn_sink, pool_w, pool_scale, w_out, final_norm_w):
    n_ctx, ctx_len, _ = x_prompt.shape
    n_lat, lat_len, _ = x_sample.shape
    past = cache_k.shape[2]
    assert n_lat + 1 <= MOD_ROWS

    cvecs = jnp.zeros((MOD_ROWS, D_MODEL), F32).at[:n_lat].set(c).at[n_lat].set(c_ctx)
    mods = _adaln(cvecs, w_ada, b_ada).reshape(DEPTH, MOD_ROWS, 3, D_MODEL)

    n_groups = len(POOL_WINDOWS)
    eye = jnp.eye(n_groups, dtype=F32)
    pool_bd = (pool_w[:, :, :, None, :] * eye[None, :, None, :, None]).reshape(DEPTH, POOL_W, POOL_W)
    vecs = jnp.zeros((DEPTH, 8, CONV_W), F32)
    vecs = vecs.at[:, 0].set(conv_b).at[:, 1].set(conv_ln_g).at[:, 2].set(conv_ln_b).at[:, 3].set(pool_scale)
    weights = (
        attn_sink,
        norm_w.reshape(DEPTH, 1, D_MODEL),
        w_in.astype(BF16),
        jnp.pad(conv_dw, ((0, 0), (0, CONV_K_PADDED - CONV_K), (0, 0))),
        vecs,
        conv_pw.astype(BF16),
        pool_bd.astype(BF16),
        w_out.astype(BF16),
        final_norm_w.reshape(1, D_MODEL),
    )

    n_seq_ctx = max(d for d in range(1, CTX_SEQS_PER_STEP + 1) if n_ctx % d == 0)
    y_prompt, new_k, new_v = _mixer_layers(x_prompt, mods, lambda s: n_lat, weights, n_seq_ctx)
    new_cache_k = new_k.reshape(n_ctx, DEPTH, ctx_len, N_KV_HEADS, HEAD_DIM)
    new_cache_v = new_v.reshape(n_ctx, DEPTH, ctx_len, N_KV_HEADS, HEAD_DIM)

    (y_sample,) = _mixer_layers(
        x_sample, mods, lambda s: s, weights, 1,
        rope=_rope_tables(lat_len),
        cache=(cache_k.reshape(n_lat, DEPTH, past, KV_W), cache_v.reshape(n_lat, DEPTH, past, KV_W)))
    return (y_prompt, y_sample, new_cache_k, new_cache_v)
```

```python
import functools

import jax
import jax.numpy as jnp
from jax import lax
from jax.experimental import pallas as pl
from jax.experimental.pallas import tpu as pltpu

D_MODEL = 1024
DEPTH = 4
GRID_W = 64
CONV_W = 256
ATTN_W = 512
POOL_W = 256
HEAD_DIM = 64
N_HEADS = 8
N_KV_HEADS = 2
GQA_GROUP = 4
KV_W = 128
WINDOW = 128
BLOCK = 128
CONV_K = 31
POOL_WINDOWS = (2, 4, 8, 16)
POOL_GROUP_W = 64
ROPE_BASE = 10000.0
EPS = 1e-6
NEG_INF = -1e30
LOG2E = 1.4426950408889634

OFF_A = 0
OFF_Q = 3 * CONV_W
OFF_KV = OFF_Q + ATTN_W
OFF_B_GATE = OFF_KV + 2 * KV_W
OFF_C = OFF_B_GATE + ATTN_W
IN_W = OFF_C + 2 * POOL_W

LANES = 128
CONV_K_PADDED = 32
CONV_PAD = 16
POOL_PAD = 8
MAX_ROW_CHUNK = 512
CONV_TILE = 32
ATTN_UNROLL = {True: 2, False: 4}
CTX_SEQS_PER_STEP = 4
ROW_SUMS_ON_MXU = {True: True, False: False}
SOFTMAX_ROWS = 32
MOD_ROWS = 16
VMEM_LIMIT_BYTES = 56 * 1024 * 1024

F32 = jnp.float32
BF16 = jnp.bfloat16


def _silu(x):
    return x * jax.nn.sigmoid(x)


def _lane_iota(shape):
    return lax.broadcasted_iota(jnp.int32, shape, 1)


def _dup_halves(x):
    r = pltpu.roll(x, HEAD_DIM, axis=1)
    lo = _lane_iota(x.shape) < HEAD_DIM
    return jnp.where(lo, x, r), jnp.where(lo, r, x)


def _with_ones(x):
    r = pltpu.roll(x, HEAD_DIM, axis=1)
    lo = _lane_iota(x.shape) < HEAD_DIM
    return jnp.where(lo, x, 1.0), jnp.where(lo, r, 1.0)


def _rope(x, cos, sin):
    quarter = HEAD_DIM // 4
    fwd = pltpu.roll(x, LANES - quarter, axis=1)
    bwd = pltpu.roll(x, quarter, axis=1)
    first = (_lane_iota(x.shape) & (2 * quarter - 1)) < quarter
    return x * cos + jnp.where(first, fwd, bwd) * sin


def _rms(x):
    return x * lax.rsqrt(jnp.mean(x * x, axis=-1, keepdims=True) + EPS)


def _adaln_kernel(c_ref, w_ref, b_ref, o_ref):
    s = _silu(c_ref[...])
    o_ref[0] = jnp.dot(s.astype(BF16), w_ref[0].astype(BF16), preferred_element_type=F32) + b_ref[0]


def _adaln(cvecs, w_ada, b_ada):
    n_col = 3
    return pl.pallas_call(
        _adaln_kernel,
        grid=(DEPTH, n_col),
        in_specs=[
            pl.BlockSpec((MOD_ROWS, D_MODEL), lambda l, j: (0, 0)),
            pl.BlockSpec((1, D_MODEL, D_MODEL), lambda l, j: (l, 0, j)),
            pl.BlockSpec((1, 1, D_MODEL), lambda l, j: (l, 0, j)),
        ],
        out_specs=pl.BlockSpec((1, MOD_ROWS, D_MODEL), lambda l, j: (l, 0, j)),
        out_shape=jax.ShapeDtypeStruct((DEPTH, MOD_ROWS, 3 * D_MODEL), F32),
        compiler_params=pltpu.CompilerParams(
            dimension_semantics=("arbitrary", "arbitrary"),
            vmem_limit_bytes=VMEM_LIMIT_BYTES),
        name="adaln_mod",
    )(cvecs, w_ada, b_ada.reshape(DEPTH, 1, 3 * D_MODEL))


def _layers_kernel(*refs, L, n_seq, row_chunk, latent):
    if latent:
        (sink_ref, x_ref, mod_ref, nw_ref, win_ref, dw_ref, vec_ref, pw_ref, pbd_ref, wout_ref,
         fnw_ref, cos_ref, sin_ref, ck_ref, cv_ref,
         o_ref,
         gpad_ref, cpad_ref, gate_ref, qm_ref, kk_ref, vv_ref, mix_ref, s_ref, p_ref, es_ref,
         ckk_ref, cvv_ref) = refs
    else:
        (sink_ref, x_ref, mod_ref, nw_ref, win_ref, dw_ref, vec_ref, pw_ref, pbd_ref, wout_ref,
         fnw_ref,
         o_ref, nk_ref, nv_ref,
         gpad_ref, cpad_ref, gate_ref, qm_ref, kk_ref, vv_ref, mix_ref, s_ref, p_ref, es_ref) = refs

    l = pl.program_id(1)
    n_chunks = n_seq * L // row_chunk
    kpad = BLOCK if latent else 0
    assert not latent or n_seq == 1

    def locate(r0):
        return divmod(r0, L)

    @pl.when(l == 0)
    def _():
        o_ref[...] = x_ref[...]

    for c in range(CONV_W // LANES):
        for s in range(n_seq):
            gpad_ref[c, s, 0:CONV_PAD, :] = jnp.zeros((CONV_PAD, LANES), F32)
            gpad_ref[c, s, CONV_PAD + L:CONV_PAD + L + CONV_PAD, :] = jnp.zeros((CONV_PAD, LANES), F32)
            cpad_ref[c, s, 0:POOL_PAD, :] = jnp.zeros((POOL_PAD, LANES), F32)
            cpad_ref[c, s, POOL_PAD + L:POOL_PAD + L + POOL_PAD, :] = jnp.zeros((POOL_PAD, LANES), F32)
    if latent:
        for g in range(N_KV_HEADS):
            for ref in (kk_ref, vv_ref):
                ref[g, 0:kpad, :] = jnp.zeros((kpad, LANES), BF16)
                ref[g, kpad + L:kpad + L + kpad, :] = jnp.zeros((kpad, LANES), BF16)
        ck0, ck1 = _dup_halves(ck_ref[0, 0])
        cv0, cv1 = _with_ones(cv_ref[0, 0]) if ROW_SUMS_ON_MXU[latent] else _dup_halves(cv_ref[0, 0])
        ckk_ref[0] = ck0.astype(BF16)
        ckk_ref[1] = ck1.astype(BF16)
        cvv_ref[0] = cv0.astype(BF16)
        cvv_ref[1] = cv1.astype(BF16)

    shift = mod_ref[0, 0, 0:1, :]
    scl = mod_ref[0, 0, 1:2, :]
    gate = mod_ref[0, 0, 2:3, :]
    norm_scale = nw_ref[0] * (1.0 + scl)

    def proj_chunk(c):
        r0 = c * row_chunk
        seq, rs = locate(r0)
        rows = pl.ds(r0, row_chunk)
        seq_rows = pl.ds(rs, row_chunk)
        x = o_ref[seq, seq_rows, :]
        h = _rms(x) * norm_scale + shift
        hb = h.astype(BF16)

        ua = jnp.dot(hb, win_ref[0, :, OFF_A:OFF_Q], preferred_element_type=F32)
        glu = ua[:, 0:CONV_W] * jax.nn.sigmoid(ua[:, CONV_W:2 * CONV_W])
        for c in range(CONV_W // LANES):
            gpad_ref[c, seq, pl.ds(rs + CONV_PAD, row_chunk), :] = glu[:, c * LANES:(c + 1) * LANES]
        gate_ref[rows, 0:CONV_W] = _silu(ua[:, 2 * CONV_W:3 * CONV_W])

        uq = jnp.dot(hb, win_ref[0, :, OFF_Q:OFF_KV], preferred_element_type=F32)
        ukv = jnp.dot(hb, win_ref[0, :, OFF_KV:OFF_B_GATE], preferred_element_type=F32)
        k = ukv[:, 0:KV_W]
        v = ukv[:, KV_W:2 * KV_W]
        if latent:
            cos = cos_ref[seq_rows, :]
            sin = sin_ref[seq_rows, :]
            k = _rope(k, cos, sin)
        else:
            nk_ref[seq, 0, seq_rows, :] = k
            nv_ref[seq, 0, seq_rows, :] = v
        lo = _lane_iota((row_chunk, LANES)) < HEAD_DIM
        scale = HEAD_DIM ** -0.5 * LOG2E
        for p in range(ATTN_W // LANES):
            qp = uq[:, p * LANES:(p + 1) * LANES]
            if latent:
                qp = _rope(qp, cos, sin)
            qp = qp * scale
            g, c = divmod(2 * p, GQA_GROUP)
            qm_ref[g, rows, c * LANES:(c + 1) * LANES] = jnp.where(lo, qp, 0.0).astype(BF16)
            qm_ref[g, rows, (c + 1) * LANES:(c + 2) * LANES] = jnp.where(lo, 0.0, qp).astype(BF16)
        k0, k1 = _dup_halves(k)
        v0, v1 = _with_ones(v) if ROW_SUMS_ON_MXU[latent] else _dup_halves(v)
        krows = pl.ds(r0 + kpad, row_chunk)
        kk_ref[0, krows, :] = k0.astype(BF16)
        kk_ref[1, krows, :] = k1.astype(BF16)
        vv_ref[0, krows, :] = v0.astype(BF16)
        vv_ref[1, krows, :] = v1.astype(BF16)

        ub = jnp.dot(hb, win_ref[0, :, OFF_B_GATE:OFF_C], preferred_element_type=F32)
        gate_ref[rows, CONV_W:CONV_W + ATTN_W] = _silu(ub)

        uc = jnp.dot(hb, win_ref[0, :, OFF_C:IN_W], preferred_element_type=F32)
        for c in range(POOL_W // LANES):
            cpad_ref[c, seq, pl.ds(rs + POOL_PAD, row_chunk), :] = uc[:, c * LANES:(c + 1) * LANES]
        gate_ref[rows, CONV_W + ATTN_W:D_MODEL] = _silu(uc[:, POOL_W:2 * POOL_W])

    conv_b = vec_ref[0, 0:1, :]
    ln_g = vec_ref[0, 1:2, :]
    ln_b = vec_ref[0, 2:3, :]
    pool_scale = vec_ref[0, 3:4, :]

    def conv_rows(r0):
        seq, rs = locate(r0)
        rows = pl.ds(r0, BLOCK)
        pieces = []
        for t in range(BLOCK // CONV_TILE):
            base = rs + t * CONV_TILE + (CONV_PAD - CONV_K // 2)
            halves = []
            for c in range(CONV_W // LANES):
                acc = jnp.zeros((CONV_TILE, LANES), F32)
                for kk in range(CONV_K):
                    acc = acc + (gpad_ref[c, seq, pl.ds(base + kk, CONV_TILE), :]
                                 * dw_ref[0, kk:kk + 1, c * LANES:(c + 1) * LANES])
                halves.append(acc)
            y = jnp.concatenate(halves, axis=1) + conv_b
            mu = jnp.mean(y, axis=-1, keepdims=True)
            yc = y - mu
            var = jnp.mean(yc * yc, axis=-1, keepdims=True)
            yn = (yc * lax.rsqrt(var + EPS)) * ln_g + ln_b
            pieces.append(_silu(yn).astype(BF16))
        z = jnp.concatenate(pieces, axis=0)
        a = jnp.dot(z, pw_ref[0], preferred_element_type=F32)
        mix_ref[rows, 0:CONV_W] = (a * gate_ref[rows, 0:CONV_W]).astype(BF16)

    def pool_rows(r0):
        seq, rs = locate(r0)
        rows = pl.ds(r0, BLOCK)
        t = rs + lax.broadcasted_iota(jnp.int32, (BLOCK, LANES), 0)
        first_group = _lane_iota((BLOCK, LANES)) < POOL_GROUP_W
        deltas = []
        for c in range(POOL_W // LANES):

            def tok(off, c=c):
                return cpad_ref[c, seq, pl.ds(rs + POOL_PAD + off, BLOCK), :]

            centre = tok(0)
            half_a, half_b = POOL_WINDOWS[2 * c] // 2, POOL_WINDOWS[2 * c + 1] // 2
            sum_a = centre
            for off in list(range(-half_a, 0)) + list(range(1, half_a)):
                sum_a = sum_a + tok(off)
            sum_b = sum_a
            for off in list(range(-half_b, -half_a)) + list(range(half_a, half_b)):
                sum_b = sum_b + tok(off)
            half_w = jnp.where(first_group, half_a, half_b)
            cnt = jnp.minimum(t + half_w, L) - jnp.maximum(t - half_w, 0)
            mean = jnp.where(first_group, sum_a, sum_b) / cnt.astype(F32)
            deltas.append((mean - centre).astype(BF16))
        d = jnp.concatenate(deltas, axis=1)
        c_out = jnp.dot(d, pbd_ref[0], preferred_element_type=F32) * pool_scale
        c0 = CONV_W + ATTN_W
        mix_ref[rows, c0:D_MODEL] = (c_out * gate_ref[rows, c0:D_MODEL]).astype(BF16)

    n_blocks = n_seq * L // BLOCK
    seq_blocks = L // BLOCK
    n_local = 3 * BLOCK if latent else L
    lo_q = _lane_iota((BLOCK, LANES)) < HEAD_DIM
    contract_last = (((1,), (1,)), ((), ()))

    def fold(pieces, op):
        acc = None
        for piece in pieces:
            for c in range(piece.shape[1] // LANES):
                slab = piece[:, c * LANES:(c + 1) * LANES]
                acc = slab if acc is None else op(acc, slab)
        return acc

    def block_rows(i):
        if isinstance(i, int):
            q0, k0 = i * BLOCK, (i // seq_blocks) * L
        else:
            q0 = pl.multiple_of(i * BLOCK, BLOCK)
            k0 = pl.multiple_of(lax.div(i, jnp.int32(seq_blocks)) * L, L)
        return q0, (pl.ds(q0, n_local) if latent else pl.ds(k0, n_local))

    def scores(i, g):
        q0, local_rows = block_rows(i)
        qm = jnp.concatenate([qm_ref[g, pl.ds(q0, BLOCK), j * LANES:(j + 1) * LANES]
                              for j in range(GQA_GROUP)], axis=0)
        s_ref[g, :, 0:n_local] = lax.dot_general(qm, kk_ref[g, local_rows, :], contract_last,
                                                 preferred_element_type=F32)
        if latent:
            s_ref[g, :, n_local:] = lax.dot_general(qm, ckk_ref[g], contract_last,
                                                    preferred_element_type=F32)

    def softmax_pv(i, g):
        q0, local_rows = block_rows(i)
        qrows = pl.ds(q0, BLOCK)
        if latent:
            jq = lax.broadcasted_iota(jnp.int32, (SOFTMAX_ROWS, BLOCK), 0)
            jk = lax.broadcasted_iota(jnp.int32, (SOFTMAX_ROWS, BLOCK), 1)
            prev_off = jnp.where(i > 0, 0, BLOCK)
            next_off = jnp.where(i < n_blocks - 1, 0, BLOCK)
        for j in range(GQA_GROUP):
            sink = sink_ref[l, g * GQA_GROUP + j] * LOG2E
            for r in range(BLOCK // SOFTMAX_ROWS):
                rr = slice(j * BLOCK + r * SOFTMAX_ROWS, j * BLOCK + (r + 1) * SOFTMAX_ROWS)
                sj = s_ref[g, rr, :]
                if latent:
                    prev_ok = jk >= jq + (prev_off + r * SOFTMAX_ROWS)
                    next_ok = jk <= jq + (r * SOFTMAX_ROWS - next_off)
                    pieces = [jnp.where(prev_ok, sj[:, 0:BLOCK], NEG_INF),
                              sj[:, BLOCK:2 * BLOCK],
                              jnp.where(next_ok, sj[:, 2 * BLOCK:3 * BLOCK], NEG_INF),
                              sj[:, n_local:]]
                else:
                    pieces = [sj]
                m = jnp.maximum(jnp.max(fold(pieces, jnp.maximum), axis=-1, keepdims=True), sink)
                e = [jnp.exp2(piece - m) for piece in pieces]
                p_ref[g, rr, :] = jnp.concatenate(e, axis=1).astype(BF16)
                tail = jnp.exp2(sink - m)
                if not ROW_SUMS_ON_MXU[latent]:
                    tail = tail + jnp.sum(fold(e, jnp.add), axis=-1, keepdims=True)
                es_ref[g, rr, :] = jnp.broadcast_to(tail, (SOFTMAX_ROWS, LANES))
        o = jnp.dot(p_ref[g, :, 0:n_local], vv_ref[g, local_rows, :],
                    preferred_element_type=F32)
        if latent:
            o = o + jnp.dot(p_ref[g, :, n_local:], cvv_ref[g], preferred_element_type=F32)
        pairs = []
        for p in range(GQA_GROUP // 2):
            rows_a = slice((2 * p) * BLOCK, (2 * p + 1) * BLOCK)
            rows_b = slice((2 * p + 1) * BLOCK, (2 * p + 2) * BLOCK)
            o_a = o[rows_a]
            o_b = o[rows_b]
            if ROW_SUMS_ON_MXU[latent]:
                inv_a = 1.0 / (o_a + es_ref[g, rows_a, :])
                inv_b = 1.0 / (o_b + es_ref[g, rows_b, :])
                pairs.append(jnp.where(lo_q, o_a * pltpu.roll(inv_a, HEAD_DIM, axis=1),
                                       pltpu.roll(o_b, HEAD_DIM, axis=1) * inv_b))
            else:
                pairs.append(jnp.where(lo_q, o_a / es_ref[g, rows_a, :], o_b / es_ref[g, rows_b, :]))
        c0 = CONV_W + g * (ATTN_W // N_KV_HEADS)
        cols = slice(c0, c0 + ATTN_W // N_KV_HEADS)
        mix_ref[qrows, cols] = (jnp.concatenate(pairs, axis=1) * gate_ref[qrows, cols]).astype(BF16)

    def attn_block(i, carry):
        scores(i, 1)
        softmax_pv(i, 0)
        scores(jnp.minimum(i + 1, n_blocks - 1), 0)
        softmax_pv(i, 1)
        return carry

    def out_chunk(c):
        seq, rs = locate(c * row_chunk)
        seq_rows = pl.ds(rs, row_chunk)
        y = jnp.dot(mix_ref[pl.ds(c * row_chunk, row_chunk), :], wout_ref[0], preferred_element_type=F32)
        o_ref[seq, seq_rows, :] = o_ref[seq, seq_rows, :] + gate * y

    def vpu_branches(c):
        for b in range(row_chunk // BLOCK):
            conv_rows(c * row_chunk + b * BLOCK)
            pool_rows(c * row_chunk + b * BLOCK)

    for c in range(n_chunks):
        proj_chunk(c)
        if c > 0:
            vpu_branches(c - 1)
    scores(0, 0)
    lax.fori_loop(0, n_blocks, attn_block, 0, unroll=min(n_blocks, ATTN_UNROLL[latent]))
    last_blocks = [(n_chunks - 1) * row_chunk + b * BLOCK for b in range(row_chunk // BLOCK)]
    for c in range(n_chunks - 1):
        out_chunk(c)
        if last_blocks:
            r0 = last_blocks.pop(0)
            conv_rows(r0)
            pool_rows(r0)
    for r0 in last_blocks:
        conv_rows(r0)
        pool_rows(r0)
    out_chunk(n_chunks - 1)

    @pl.when(l == DEPTH - 1)
    def _():
        for c in range(n_chunks):
            seq, rs = locate(c * row_chunk)
            seq_rows = pl.ds(rs, row_chunk)
            o_ref[seq, seq_rows, :] = _rms(o_ref[seq, seq_rows, :]) * fnw_ref[...]


def _mixer_layers(x, mods, mod_row, weights, n_seq, rope=None, cache=None):
    S, L, _ = x.shape
    latent = cache is not None
    sink, norm_w, w_in, conv_dw, vecs, conv_pw, pool_bd, w_out, final_norm_w = weights
    kpad = BLOCK if latent else 0
    n_keys = 3 * BLOCK + cache[0].shape[2] if latent else L
    R = n_seq * L
    row_chunk = min(MAX_ROW_CHUNK, L)
    assert S % n_seq == 0 and L % row_chunk == 0 and row_chunk % BLOCK == 0

    per_layer = lambda *tail: (lambda s, l: (l,) + tail)
    in_specs = [
        pl.BlockSpec(memory_space=pltpu.SMEM),
        pl.BlockSpec((n_seq, L, D_MODEL), lambda s, l: (s, 0, 0)),
        pl.BlockSpec((1, 1, 3, D_MODEL), lambda s, l: (l, mod_row(s), 0, 0)),
        pl.BlockSpec((1, 1, D_MODEL), per_layer(0, 0)),
        pl.BlockSpec((1, D_MODEL, IN_W), per_layer(0, 0)),
        pl.BlockSpec((1, CONV_K_PADDED, CONV_W), per_layer(0, 0)),
        pl.BlockSpec((1, 8, CONV_W), per_layer(0, 0)),
        pl.BlockSpec((1, CONV_W, CONV_W), per_layer(0, 0)),
        pl.BlockSpec((1, POOL_W, POOL_W), per_layer(0, 0)),
        pl.BlockSpec((1, D_MODEL, D_MODEL), per_layer(0, 0)),
        pl.BlockSpec((1, D_MODEL), lambda s, l: (0, 0)),
    ]
    args = [sink, x, mods, norm_w, w_in, conv_dw, vecs, conv_pw, pool_bd, w_out, final_norm_w]
    out_specs = [pl.BlockSpec((n_seq, L, D_MODEL), lambda s, l: (s, 0, 0))]
    out_shape = [jax.ShapeDtypeStruct((S, L, D_MODEL), F32)]
    scratch = [
        pltpu.VMEM((CONV_W // LANES, n_seq, L + 2 * CONV_PAD, LANES), F32),
        pltpu.VMEM((POOL_W // LANES, n_seq, L + 2 * POOL_PAD, LANES), F32),
        pltpu.VMEM((R, D_MODEL), F32),
        pltpu.VMEM((N_KV_HEADS, R, ATTN_W), BF16),
        pltpu.VMEM((N_KV_HEADS, R + 2 * kpad, LANES), BF16),
        pltpu.VMEM((N_KV_HEADS, R + 2 * kpad, LANES), BF16),
        pltpu.VMEM((R, D_MODEL), BF16),
        pltpu.VMEM((N_KV_HEADS, GQA_GROUP * BLOCK, n_keys), F32),
        pltpu.VMEM((N_KV_HEADS, GQA_GROUP * BLOCK, n_keys), BF16),
        pltpu.VMEM((N_KV_HEADS, GQA_GROUP * BLOCK, LANES), F32),
    ]
    if latent:
        cos, sin = rope
        ck, cv = cache
        past = ck.shape[2]
        in_specs += [
            pl.BlockSpec((L, LANES), lambda s, l: (0, 0)),
            pl.BlockSpec((L, LANES), lambda s, l: (0, 0)),
            pl.BlockSpec((1, 1, past, KV_W), lambda s, l: (s, l, 0, 0)),
            pl.BlockSpec((1, 1, past, KV_W), lambda s, l: (s, l, 0, 0)),
        ]
        args += [cos, sin, ck, cv]
        scratch += [pltpu.VMEM((N_KV_HEADS, past, LANES), BF16),
                    pltpu.VMEM((N_KV_HEADS, past, LANES), BF16)]
    else:
        out_specs += [pl.BlockSpec((n_seq, 1, L, KV_W), lambda s, l: (s, l, 0, 0))] * 2
        out_shape += [jax.ShapeDtypeStruct((S, DEPTH, L, KV_W), F32)] * 2

    return pl.pallas_call(
        functools.partial(_layers_kernel, L=L, n_seq=n_seq, row_chunk=row_chunk, latent=latent),
        grid=(S // n_seq, DEPTH),
        in_specs=in_specs,
        out_specs=out_specs,
        out_shape=out_shape,
        scratch_shapes=scratch,
        compiler_params=pltpu.CompilerParams(
            dimension_semantics=("arbitrary", "arbitrary"),
            vmem_limit_bytes=VMEM_LIMIT_BYTES),
        name="latent_layers" if latent else "context_layers",
    )(*args)


def _rope_tables(L):
    quarter = HEAD_DIM // 4
    t = jnp.arange(L)
    rows = (t // GRID_W).astype(F32)
    cols = (t % GRID_W).astype(F32)
    freqs = ROPE_BASE ** (-jnp.arange(quarter, dtype=F32) / quarter)
    ang_r = rows[:, None] * freqs[None, :]
    ang_c = cols[:, None] * freqs[None, :]
    cos = jnp.concatenate([jnp.cos(ang_r)] * 2 + [jnp.cos(ang_c)] * 2, axis=-1)
    sin = jnp.concatenate([-jnp.sin(ang_r), jnp.sin(ang_r), -jnp.sin(ang_c), jnp.sin(ang_c)], axis=-1)
    return jnp.tile(cos, (1, 2)), jnp.tile(sin, (1, 2))


def kernel(x_prompt, x_sample, c, cache_k, cache_v, c_ctx, w_ada, b_ada, norm_w, w_in, conv_dw, conv_b,
           conv_ln_g, conv_ln_b, conv_pw, attn_sink, pool_w, pool_scale, w_out, final_norm_w):
    n_ctx, ctx_len, _ = x_prompt.shape
    n_lat, lat_len, _ = x_sample.shape
    past = cache_k.shape[2]
    assert n_lat + 1 <= MOD_ROWS

    cvecs = jnp.zeros((MOD_ROWS, D_MODEL), F32).at[:n_lat].set(c).at[n_lat].set(c_ctx)
    mods = _adaln(cvecs, w_ada, b_ada).reshape(DEPTH, MOD_ROWS, 3, D_MODEL)

    n_groups = len(POOL_WINDOWS)
    eye = jnp.eye(n_groups, dtype=F32)
    pool_bd = (pool_w[:, :, :, None, :] * eye[None, :, None, :, None]).reshape(DEPTH, POOL_W, POOL_W)
    vecs = jnp.zeros((DEPTH, 8, CONV_W), F32)
    vecs = vecs.at[:, 0].set(conv_b).at[:, 1].set(conv_ln_g).at[:, 2].set(conv_ln_b).at[:, 3].set(pool_scale)
    weights = (
        attn_sink,
        norm_w.reshape(DEPTH, 1, D_MODEL),
        w_in.astype(BF16),
        jnp.pad(conv_dw, ((0, 0), (0, CONV_K_PADDED - CONV_K), (0, 0))),
        vecs,
        conv_pw.astype(BF16),
        pool_bd.astype(BF16),
        w_out.astype(BF16),
        final_norm_w.reshape(1, D_MODEL),
    )

    n_seq_ctx = max(d for d in range(1, CTX_SEQS_PER_STEP + 1) if n_ctx % d == 0)
    y_prompt, new_k, new_v = _mixer_layers(x_prompt, mods, lambda s: n_lat, weights, n_seq_ctx)
    new_cache_k = new_k.reshape(n_ctx, DEPTH, ctx_len, N_KV_HEADS, HEAD_DIM)
    new_cache_v = new_v.reshape(n_ctx, DEPTH, ctx_len, N_KV_HEADS, HEAD_DIM)

    (y_sample,) = _mixer_layers(
        x_sample, mods, lambda s: s, weights, 1,
        rope=_rope_tables(lat_len),
        cache=(cache_k.reshape(n_lat, DEPTH, past, KV_W), cache_v.reshape(n_lat, DEPTH, past, KV_W)))
    return (y_prompt, y_sample, new_cache_k, new_cache_v)
```

```python
import functools

import jax
import jax.numpy as jnp
from jax import lax
from jax.experimental import pallas as pl
from jax.experimental.pallas import tpu as pltpu

D_MODEL = 1024
DEPTH = 4
GRID_W = 64
CONV_W = 256
ATTN_W = 512
POOL_W = 256
HEAD_DIM = 64
N_HEADS = 8
N_KV_HEADS = 2
GQA_GROUP = 4
KV_W = 128
WINDOW = 128
BLOCK = 128
CONV_K = 31
POOL_WINDOWS = (2, 4, 8, 16)
POOL_GROUP_W = 64
ROPE_BASE = 10000.0
EPS = 1e-6
NEG_INF = -1e30
LOG2E = 1.4426950408889634

OFF_A = 0
OFF_Q = 3 * CONV_W
OFF_KV = OFF_Q + ATTN_W
OFF_B_GATE = OFF_KV + 2 * KV_W
OFF_C = OFF_B_GATE + ATTN_W
IN_W = OFF_C + 2 * POOL_W

LANES = 128
CONV_K_PADDED = 32
CONV_PAD = 16
POOL_PAD = 8
MAX_ROW_CHUNK = 512
CONV_TILE = 32
ATTN_UNROLL = {True: 2, False: 4}
CTX_SEQS_PER_STEP = 1
ROW_SUMS_ON_MXU = {True: True, False: False}
SOFTMAX_ROWS = 32
MOD_ROWS = 16
VMEM_LIMIT_BYTES = 56 * 1024 * 1024

F32 = jnp.float32
BF16 = jnp.bfloat16


def _silu(x):
    return x * jax.nn.sigmoid(x)


def _lane_iota(shape):
    return lax.broadcasted_iota(jnp.int32, shape, 1)


def _dup_halves(x):
    r = pltpu.roll(x, HEAD_DIM, axis=1)
    lo = _lane_iota(x.shape) < HEAD_DIM
    return jnp.where(lo, x, r), jnp.where(lo, r, x)


def _with_ones(x):
    r = pltpu.roll(x, HEAD_DIM, axis=1)
    lo = _lane_iota(x.shape) < HEAD_DIM
    return jnp.where(lo, x, 1.0), jnp.where(lo, r, 1.0)


def _rope(x, cos, sin):
    quarter = HEAD_DIM // 4
    fwd = pltpu.roll(x, LANES - quarter, axis=1)
    bwd = pltpu.roll(x, quarter, axis=1)
    first = (_lane_iota(x.shape) & (2 * quarter - 1)) < quarter
    return x * cos + jnp.where(first, fwd, bwd) * sin


def _rms(x):
    return x * lax.rsqrt(jnp.mean(x * x, axis=-1, keepdims=True) + EPS)


def _adaln_kernel(c_ref, w_ref, b_ref, o_ref):
    s = _silu(c_ref[...])
    o_ref[0] = jnp.dot(s.astype(BF16), w_ref[0].astype(BF16), preferred_element_type=F32) + b_ref[0]


def _adaln(cvecs, w_ada, b_ada):
    return pl.pallas_call(
        _adaln_kernel,
        grid=(DEPTH,),
        in_specs=[
            pl.BlockSpec((MOD_ROWS, D_MODEL), lambda l: (0, 0)),
            pl.BlockSpec((1, D_MODEL, 3 * D_MODEL), lambda l: (l, 0, 0)),
            pl.BlockSpec((1, 1, 3 * D_MODEL), lambda l: (l, 0, 0)),
        ],
        out_specs=pl.BlockSpec((1, MOD_ROWS, 3 * D_MODEL), lambda l: (l, 0, 0)),
        out_shape=jax.ShapeDtypeStruct((DEPTH, MOD_ROWS, 3 * D_MODEL), F32),
        compiler_params=pltpu.CompilerParams(
            dimension_semantics=("arbitrary",),
            vmem_limit_bytes=VMEM_LIMIT_BYTES),
        name="adaln_mod",
    )(cvecs, w_ada, b_ada.reshape(DEPTH, 1, 3 * D_MODEL))


def _layers_kernel(*refs, L, n_seq, row_chunk, latent):
    if latent:
        (sink_ref, x_ref, mod_ref, nw_ref, win_ref, dw_ref, vec_ref, pw_ref, pbd_ref, wout_ref,
         fnw_ref, cos_ref, sin_ref, ck_ref, cv_ref,
         o_ref,
         gpad_ref, cpad_ref, gate_ref, qm_ref, kk_ref, vv_ref, mix_ref, s_ref, p_ref, es_ref,
         ckk_ref, cvv_ref) = refs
    else:
        (sink_ref, x_ref, mod_ref, nw_ref, win_ref, dw_ref, vec_ref, pw_ref, pbd_ref, wout_ref,
         fnw_ref,
         o_ref, nk_ref, nv_ref,
         gpad_ref, cpad_ref, gate_ref, qm_ref, kk_ref, vv_ref, mix_ref, s_ref, p_ref, es_ref) = refs

    l = pl.program_id(1)
    n_chunks = n_seq * L // row_chunk
    kpad = BLOCK if latent else 0
    assert not latent or n_seq == 1

    def locate(r0):
        return divmod(r0, L)

    @pl.when(l == 0)
    def _():
        o_ref[...] = x_ref[...]

    for c in range(CONV_W // LANES):
        for s in range(n_seq):
            gpad_ref[c, s, 0:CONV_PAD, :] = jnp.zeros((CONV_PAD, LANES), F32)
            gpad_ref[c, s, CONV_PAD + L:CONV_PAD + L + CONV_PAD, :] = jnp.zeros((CONV_PAD, LANES), F32)
            cpad_ref[c, s, 0:POOL_PAD, :] = jnp.zeros((POOL_PAD, LANES), F32)
            cpad_ref[c, s, POOL_PAD + L:POOL_PAD + L + POOL_PAD, :] = jnp.zeros((POOL_PAD, LANES), F32)
    if latent:
        for g in range(N_KV_HEADS):
            for ref in (kk_ref, vv_ref):
                ref[g, 0:kpad, :] = jnp.zeros((kpad, LANES), BF16)
                ref[g, kpad + L:kpad + L + kpad, :] = jnp.zeros((kpad, LANES), BF16)
        ck0, ck1 = _dup_halves(ck_ref[0, 0])
        cv0, cv1 = _with_ones(cv_ref[0, 0]) if ROW_SUMS_ON_MXU[latent] else _dup_halves(cv_ref[0, 0])
        ckk_ref[0] = ck0.astype(BF16)
        ckk_ref[1] = ck1.astype(BF16)
        cvv_ref[0] = cv0.astype(BF16)
        cvv_ref[1] = cv1.astype(BF16)

    shift = mod_ref[0, 0, 0:1, :]
    scl = mod_ref[0, 0, 1:2, :]
    gate = mod_ref[0, 0, 2:3, :]
    norm_scale = nw_ref[0] * (1.0 + scl)

    def proj_chunk(c):
        r0 = c * row_chunk
        seq, rs = locate(r0)
        rows = pl.ds(r0, row_chunk)
        seq_rows = pl.ds(rs, row_chunk)
        x = o_ref[seq, seq_rows, :]
        h = _rms(x) * norm_scale + shift
        hb = h.astype(BF16)

        ua = jnp.dot(hb, win_ref[0, :, OFF_A:OFF_Q], preferred_element_type=F32)
        glu = ua[:, 0:CONV_W] * jax.nn.sigmoid(ua[:, CONV_W:2 * CONV_W])
        for c in range(CONV_W // LANES):
            gpad_ref[c, seq, pl.ds(rs + CONV_PAD, row_chunk), :] = glu[:, c * LANES:(c + 1) * LANES]
        gate_ref[rows, 0:CONV_W] = _silu(ua[:, 2 * CONV_W:3 * CONV_W])

        uq = jnp.dot(hb, win_ref[0, :, OFF_Q:OFF_KV], preferred_element_type=F32)
        ukv = jnp.dot(hb, win_ref[0, :, OFF_KV:OFF_B_GATE], preferred_element_type=F32)
        k = ukv[:, 0:KV_W]
        v = ukv[:, KV_W:2 * KV_W]
        if latent:
            cos = cos_ref[seq_rows, :]
            sin = sin_ref[seq_rows, :]
            k = _rope(k, cos, sin)
        else:
            nk_ref[seq, 0, seq_rows, :] = k
            nv_ref[seq, 0, seq_rows, :] = v
        lo = _lane_iota((row_chunk, LANES)) < HEAD_DIM
        scale = HEAD_DIM ** -0.5 * LOG2E
        for p in range(ATTN_W // LANES):
            qp = uq[:, p * LANES:(p + 1) * LANES]
            if latent:
                qp = _rope(qp, cos, sin)
            qp = qp * scale
            g, c = divmod(2 * p, GQA_GROUP)
            qm_ref[g, rows, c * LANES:(c + 1) * LANES] = jnp.where(lo, qp, 0.0).astype(BF16)
            qm_ref[g, rows, (c + 1) * LANES:(c + 2) * LANES] = jnp.where(lo, 0.0, qp).astype(BF16)
        k0, k1 = _dup_halves(k)
        v0, v1 = _with_ones(v) if ROW_SUMS_ON_MXU[latent] else _dup_halves(v)
        krows = pl.ds(r0 + kpad, row_chunk)
        kk_ref[0, krows, :] = k0.astype(BF16)
        kk_ref[1, krows, :] = k1.astype(BF16)
        vv_ref[0, krows, :] = v0.astype(BF16)
        vv_ref[1, krows, :] = v1.astype(BF16)

        ub = jnp.dot(hb, win_ref[0, :, OFF_B_GATE:OFF_C], preferred_element_type=F32)
        gate_ref[rows, CONV_W:CONV_W + ATTN_W] = _silu(ub)

        uc = jnp.dot(hb, win_ref[0, :, OFF_C:IN_W], preferred_element_type=F32)
        for c in range(POOL_W // LANES):
            cpad_ref[c, seq, pl.ds(rs + POOL_PAD, row_chunk), :] = uc[:, c * LANES:(c + 1) * LANES]
        gate_ref[rows, CONV_W + ATTN_W:D_MODEL] = _silu(uc[:, POOL_W:2 * POOL_W])

    conv_b = vec_ref[0, 0:1, :]
    ln_g = vec_ref[0, 1:2, :]
    ln_b = vec_ref[0, 2:3, :]
    pool_scale = vec_ref[0, 3:4, :]

    def conv_rows(r0):
        seq, rs = locate(r0)
        rows = pl.ds(r0, BLOCK)
        pieces = []
        for t in range(BLOCK // CONV_TILE):
            base = rs + t * CONV_TILE + (CONV_PAD - CONV_K // 2)
            halves = []
            for c in range(CONV_W // LANES):
                acc = jnp.zeros((CONV_TILE, LANES), F32)
                for kk in range(CONV_K):
                    acc = acc + (gpad_ref[c, seq, pl.ds(base + kk, CONV_TILE), :]
                                 * dw_ref[0, kk:kk + 1, c * LANES:(c + 1) * LANES])
                halves.append(acc)
            y = jnp.concatenate(halves, axis=1) + conv_b
            mu = jnp.mean(y, axis=-1, keepdims=True)
            yc = y - mu
            var = jnp.mean(yc * yc, axis=-1, keepdims=True)
            yn = (yc * lax.rsqrt(var + EPS)) * ln_g + ln_b
            pieces.append(_silu(yn).astype(BF16))
        z = jnp.concatenate(pieces, axis=0)
        a = jnp.dot(z, pw_ref[0], preferred_element_type=F32)
        mix_ref[rows, 0:CONV_W] = (a * gate_ref[rows, 0:CONV_W]).astype(BF16)

    def pool_rows(r0):
        seq, rs = locate(r0)
        rows = pl.ds(r0, BLOCK)
        t = rs + lax.broadcasted_iota(jnp.int32, (BLOCK, LANES), 0)
        first_group = _lane_iota((BLOCK, LANES)) < POOL_GROUP_W
        deltas = []
        for c in range(POOL_W // LANES):

            def tok(off, c=c):
                return cpad_ref[c, seq, pl.ds(rs + POOL_PAD + off, BLOCK), :]

            centre = tok(0)
            half_a, half_b = POOL_WINDOWS[2 * c] // 2, POOL_WINDOWS[2 * c + 1] // 2
            sum_a = centre
            for off in list(range(-half_a, 0)) + list(range(1, half_a)):
                sum_a = sum_a + tok(off)
            sum_b = sum_a
            for off in list(range(-half_b, -half_a)) + list(range(half_a, half_b)):
                sum_b = sum_b + tok(off)
            half_w = jnp.where(first_group, half_a, half_b)
            cnt = jnp.minimum(t + half_w, L) - jnp.maximum(t - half_w, 0)
            mean = jnp.where(first_group, sum_a, sum_b) / cnt.astype(F32)
            deltas.append((mean - centre).astype(BF16))
        d = jnp.concatenate(deltas, axis=1)
        c_out = jnp.dot(d, pbd_ref[0], preferred_element_type=F32) * pool_scale
        c0 = CONV_W + ATTN_W
        mix_ref[rows, c0:D_MODEL] = (c_out * gate_ref[rows, c0:D_MODEL]).astype(BF16)

    n_blocks = n_seq * L // BLOCK
    seq_blocks = L // BLOCK
    n_local = 3 * BLOCK if latent else L
    lo_q = _lane_iota((BLOCK, LANES)) < HEAD_DIM
    contract_last = (((1,), (1,)), ((), ()))

    def fold(pieces, op):
        acc = None
        for piece in pieces:
            for c in range(piece.shape[1] // LANES):
                slab = piece[:, c * LANES:(c + 1) * LANES]
                acc = slab if acc is None else op(acc, slab)
        return acc

    def block_rows(i):
        if isinstance(i, int):
            q0, k0 = i * BLOCK, (i // seq_blocks) * L
        else:
            q0 = pl.multiple_of(i * BLOCK, BLOCK)
            k0 = pl.multiple_of(lax.div(i, jnp.int32(seq_blocks)) * L, L)
        return q0, (pl.ds(q0, n_local) if latent else pl.ds(k0, n_local))

    def scores(i, g):
        q0, local_rows = block_rows(i)
        qm = jnp.concatenate([qm_ref[g, pl.ds(q0, BLOCK), j * LANES:(j + 1) * LANES]
                              for j in range(GQA_GROUP)], axis=0)
        s_ref[g, :, 0:n_local] = lax.dot_general(qm, kk_ref[g, local_rows, :], contract_last,
                                                 preferred_element_type=F32)
        if latent:
            s_ref[g, :, n_local:] = lax.dot_general(qm, ckk_ref[g], contract_last,
                                                    preferred_element_type=F32)

    def softmax_pv(i, g):
        q0, local_rows = block_rows(i)
        qrows = pl.ds(q0, BLOCK)
        if latent:
            jq = lax.broadcasted_iota(jnp.int32, (SOFTMAX_ROWS, BLOCK), 0)
            jk = lax.broadcasted_iota(jnp.int32, (SOFTMAX_ROWS, BLOCK), 1)
            prev_off = jnp.where(i > 0, 0, BLOCK)
            next_off = jnp.where(i < n_blocks - 1, 0, BLOCK)
        for j in range(GQA_GROUP):
            sink = sink_ref[l, g * GQA_GROUP + j] * LOG2E
            for r in range(BLOCK // SOFTMAX_ROWS):
                rr = slice(j * BLOCK + r * SOFTMAX_ROWS, j * BLOCK + (r + 1) * SOFTMAX_ROWS)
                sj = s_ref[g, rr, :]
                if latent:
                    prev_ok = jk >= jq + (prev_off + r * SOFTMAX_ROWS)
                    next_ok = jk <= jq + (r * SOFTMAX_ROWS - next_off)
                    pieces = [jnp.where(prev_ok, sj[:, 0:BLOCK], NEG_INF),
                              sj[:, BLOCK:2 * BLOCK],
                              jnp.where(next_ok, sj[:, 2 * BLOCK:3 * BLOCK], NEG_INF),
                              sj[:, n_local:]]
                else:
                    pieces = [sj]
                m = jnp.maximum(jnp.max(fold(pieces, jnp.maximum), axis=-1, keepdims=True), sink)
                e = [jnp.exp2(piece - m) for piece in pieces]
                p_ref[g, rr, :] = jnp.concatenate(e, axis=1).astype(BF16)
                tail = jnp.exp2(sink - m)
                if not ROW_SUMS_ON_MXU[latent]:
                    tail = tail + jnp.sum(fold(e, jnp.add), axis=-1, keepdims=True)
                es_ref[g, rr, :] = jnp.broadcast_to(tail, (SOFTMAX_ROWS, LANES))
        o = jnp.dot(p_ref[g, :, 0:n_local], vv_ref[g, local_rows, :],
                    preferred_element_type=F32)
        if latent:
            o = o + jnp.dot(p_ref[g, :, n_local:], cvv_ref[g], preferred_element_type=F32)
        pairs = []
        for p in range(GQA_GROUP // 2):
            rows_a = slice((2 * p) * BLOCK, (2 * p + 1) * BLOCK)
            rows_b = slice((2 * p + 1) * BLOCK, (2 * p + 2) * BLOCK)
            o_a = o[rows_a]
            o_b = o[rows_b]
            if ROW_SUMS_ON_MXU[latent]:
                inv_a = 1.0 / (o_a + es_ref[g, rows_a, :])
                inv_b = 1.0 / (o_b + es_ref[g, rows_b, :])
                pairs.append(jnp.where(lo_q, o_a * pltpu.roll(inv_a, HEAD_DIM, axis=1),
                                       pltpu.roll(o_b, HEAD_DIM, axis=1) * inv_b))
            else:
                pairs.append(jnp.where(lo_q, o_a / es_ref[g, rows_a, :], o_b / es_ref[g, rows_b, :]))
        c0 = CONV_W + g * (ATTN_W // N_KV_HEADS)
        cols = slice(c0, c0 + ATTN_W // N_KV_HEADS)
        mix_ref[qrows, cols] = (jnp.concatenate(pairs, axis=1) * gate_ref[qrows, cols]).astype(BF16)

    def attn_block(i, carry):
        scores(i, 1)
        softmax_pv(i, 0)
        scores(jnp.minimum(i + 1, n_blocks - 1), 0)
        softmax_pv(i, 1)
        return carry

    def out_chunk(c):
        seq, rs = locate(c * row_chunk)
        seq_rows = pl.ds(rs, row_chunk)
        y = jnp.dot(mix_ref[pl.ds(c * row_chunk, row_chunk), :], wout_ref[0], preferred_element_type=F32)
        o_ref[seq, seq_rows, :] = o_ref[seq, seq_rows, :] + gate * y

    def vpu_branches(c):
        for b in range(row_chunk // BLOCK):
            conv_rows(c * row_chunk + b * BLOCK)
            pool_rows(c * row_chunk + b * BLOCK)

    for c in range(n_chunks):
        proj_chunk(c)
        if c > 0:
            vpu_branches(c - 1)
    scores(0, 0)
    lax.fori_loop(0, n_blocks, attn_block, 0, unroll=min(n_blocks, ATTN_UNROLL[latent]))
    last_blocks = [(n_chunks - 1) * row_chunk + b * BLOCK for b in range(row_chunk // BLOCK)]
    for c in range(n_chunks - 1):
        out_chunk(c)
        if last_blocks:
            r0 = last_blocks.pop(0)
            conv_rows(r0)
            pool_rows(r0)
    for r0 in last_blocks:
        conv_rows(r0)
        pool_rows(r0)
    out_chunk(n_chunks - 1)

    @pl.when(l == DEPTH - 1)
    def _():
        for c in range(n_chunks):
            seq, rs = locate(c * row_chunk)
            seq_rows = pl.ds(rs, row_chunk)
            o_ref[seq, seq_rows, :] = _rms(o_ref[seq, seq_rows, :]) * fnw_ref[...]


def _mixer_layers(x, mods, mod_row, weights, n_seq, rope=None, cache=None):
    S, L, _ = x.shape
    latent = cache is not None
    sink, norm_w, w_in, conv_dw, vecs, conv_pw, pool_bd, w_out, final_norm_w = weights
    kpad = BLOCK if latent else 0
    n_keys = 3 * BLOCK + cache[0].shape[2] if latent else L
    R = n_seq * L
    row_chunk = min(MAX_ROW_CHUNK, L)
    assert S % n_seq == 0 and L % row_chunk == 0 and row_chunk % BLOCK == 0

    per_layer = lambda *tail: (lambda s, l: (l,) + tail)
    in_specs = [
        pl.BlockSpec(memory_space=pltpu.SMEM),
        pl.BlockSpec((n_seq, L, D_MODEL), lambda s, l: (s, 0, 0)),
        pl.BlockSpec((1, 1, 3, D_MODEL), lambda s, l: (l, mod_row(s), 0, 0)),
        pl.BlockSpec((1, 1, D_MODEL), per_layer(0, 0)),
        pl.BlockSpec((1, D_MODEL, IN_W), per_layer(0, 0)),
        pl.BlockSpec((1, CONV_K_PADDED, CONV_W), per_layer(0, 0)),
        pl.BlockSpec((1, 8, CONV_W), per_layer(0, 0)),
        pl.BlockSpec((1, CONV_W, CONV_W), per_layer(0, 0)),
        pl.BlockSpec((1, POOL_W, POOL_W), per_layer(0, 0)),
        pl.BlockSpec((1, D_MODEL, D_MODEL), per_layer(0, 0)),
        pl.BlockSpec((1, D_MODEL), lambda s, l: (0, 0)),
    ]
    args = [sink, x, mods, norm_w, w_in, conv_dw, vecs, conv_pw, pool_bd, w_out, final_norm_w]
    out_specs = [pl.BlockSpec((n_seq, L, D_MODEL), lambda s, l: (s, 0, 0))]
    out_shape = [jax.ShapeDtypeStruct((S, L, D_MODEL), F32)]
    scratch = [
        pltpu.VMEM((CONV_W // LANES, n_seq, L + 2 * CONV_PAD, LANES), F32),
        pltpu.VMEM((POOL_W // LANES, n_seq, L + 2 * POOL_PAD, LANES), F32),
        pltpu.VMEM((R, D_MODEL), F32),
        pltpu.VMEM((N_KV_HEADS, R, ATTN_W), BF16),
        pltpu.VMEM((N_KV_HEADS, R + 2 * kpad, LANES), BF16),
        pltpu.VMEM((N_KV_HEADS, R + 2 * kpad, LANES), BF16),
        pltpu.VMEM((R, D_MODEL), BF16),
        pltpu.VMEM((N_KV_HEADS, GQA_GROUP * BLOCK, n_keys), F32),
        pltpu.VMEM((N_KV_HEADS, GQA_GROUP * BLOCK, n_keys), BF16),
        pltpu.VMEM((N_KV_HEADS, GQA_GROUP * BLOCK, LANES), F32),
    ]
    if latent:
        cos, sin = rope
        ck, cv = cache
        past = ck.shape[2]
        in_specs += [
            pl.BlockSpec((L, LANES), lambda s, l: (0, 0)),
            pl.BlockSpec((L, LANES), lambda s, l: (0, 0)),
            pl.BlockSpec((1, 1, past, KV_W), lambda s, l: (s, l, 0, 0)),
            pl.BlockSpec((1, 1, past, KV_W), lambda s, l: (s, l, 0, 0)),
        ]
        args += [cos, sin, ck, cv]
        scratch += [pltpu.VMEM((N_KV_HEADS, past, LANES), BF16),
                    pltpu.VMEM((N_KV_HEADS, past, LANES), BF16)]
    else:
        out_specs += [pl.BlockSpec((n_seq, 1, L, KV_W), lambda s, l: (s, l, 0, 0))] * 2
        out_shape += [jax.ShapeDtypeStruct((S, DEPTH, L, KV_W), F32)] * 2

    return pl.pallas_call(
        functools.partial(_layers_kernel, L=L, n_seq=n_seq, row_chunk=row_chunk, latent=latent),
        grid=(S // n_seq, DEPTH),
        in_specs=in_specs,
        out_specs=out_specs,
        out_shape=out_shape,
        scratch_shapes=scratch,
        compiler_params=pltpu.CompilerParams(
            dimension_semantics=("arbitrary", "arbitrary"),
            vmem_limit_bytes=VMEM_LIMIT_BYTES),
        name="latent_layers" if latent else "context_layers",
    )(*args)


def _rope_tables(L):
    quarter = HEAD_DIM // 4
    t = jnp.arange(L)
    rows = (t // GRID_W).astype(F32)
    cols = (t % GRID_W).astype(F32)
    freqs = ROPE_BASE ** (-jnp.arange(quarter, dtype=F32) / quarter)
    ang_r = rows[:, None] * freqs[None, :]
    ang_c = cols[:, None] * freqs[None, :]
    cos = jnp.concatenate([jnp.cos(ang_r)] * 2 + [jnp.cos(ang_c)] * 2, axis=-1)
    sin = jnp.concatenate([-jnp.sin(ang_r), jnp.sin(ang_r), -jnp.sin(ang_c), jnp.sin(ang_c)], axis=-1)
    return jnp.tile(cos, (1, 2)), jnp.tile(sin, (1, 2))


def kernel(x_prompt, x_sample, c, cache_k, cache_v, c_ctx, w_ada, b_ada, norm_w, w_in, conv_dw, conv_b,
           conv_ln_g, conv_ln_b, conv_pw, attn_sink, pool_w, pool_scale, w_out, final_norm_w):
    n_ctx, ctx_len, _ = x_prompt.shape
    n_lat, lat_len, _ = x_sample.shape
    past = cache_k.shape[2]
    assert n_lat + 1 <= MOD_ROWS

    cvecs = jnp.zeros((MOD_ROWS, D_MODEL), F32).at[:n_lat].set(c).at[n_lat].set(c_ctx)
    mods = _adaln(cvecs, w_ada, b_ada).reshape(DEPTH, MOD_ROWS, 3, D_MODEL)

    n_groups = len(POOL_WINDOWS)
    eye = jnp.eye(n_groups, dtype=F32)
    pool_bd = (pool_w[:, :, :, None, :] * eye[None, :, None, :, None]).reshape(DEPTH, POOL_W, POOL_W)
    vecs = jnp.zeros((DEPTH, 8, CONV_W), F32)
    vecs = vecs.at[:, 0].set(conv_b).at[:, 1].set(conv_ln_g).at[:, 2].set(conv_ln_b).at[:, 3].set(pool_scale)
    weights = (
        attn_sink,
        norm_w.reshape(DEPTH, 1, D_MODEL),
        w_in.astype(BF16),
        jnp.pad(conv_dw, ((0, 0), (0, CONV_K_PADDED - CONV_K), (0, 0))),
        vecs,
        conv_pw.astype(BF16),
        pool_bd.astype(BF16),
        w_out.astype(BF16),
        final_norm_w.reshape(1, D_MODEL),
    )

    n_seq_ctx = max(d for d in range(1, CTX_SEQS_PER_STEP + 1) if n_ctx % d == 0)
    y_prompt, new_k, new_v = _mixer_layers(x_prompt, mods, lambda s: n_lat, weights, n_seq_ctx)
    new_cache_k = new_k.reshape(n_ctx, DEPTH, ctx_len, N_KV_HEADS, HEAD_DIM)
    new_cache_v = new_v.reshape(n_ctx, DEPTH, ctx_len, N_KV_HEADS, HEAD_DIM)

    (y_sample,) = _mixer_layers(
        x_sample, mods, lambda s: s, weights, 1,
        rope=_rope_tables(lat_len),
        cache=(cache_k.reshape(n_lat, DEPTH, past, KV_W), cache_v.reshape(n_lat, DEPTH, past, KV_W)))
    return (y_prompt, y_sample, new_cache_k, new_cache_v)
```

```python
import functools

import jax
import jax.numpy as jnp
from jax import lax
from jax.experimental import pallas as pl
from jax.experimental.pallas import tpu as pltpu

D_MODEL = 1024
DEPTH = 4
GRID_W = 64
CONV_W = 256
ATTN_W = 512
POOL_W = 256
HEAD_DIM = 64
N_HEADS = 8
N_KV_HEADS = 2
GQA_GROUP = 4
KV_W = 128
WINDOW = 128
BLOCK = 128
CONV_K = 31
POOL_WINDOWS = (2, 4, 8, 16)
POOL_GROUP_W = 64
ROPE_BASE = 10000.0
EPS = 1e-6
NEG_INF = -1e30
LOG2E = 1.4426950408889634

OFF_A = 0
OFF_Q = 3 * CONV_W
OFF_KV = OFF_Q + ATTN_W
OFF_B_GATE = OFF_KV + 2 * KV_W
OFF_C = OFF_B_GATE + ATTN_W
IN_W = OFF_C + 2 * POOL_W

LANES = 128
CONV_K_PADDED = 32
CONV_PAD = 16
POOL_PAD = 8
MAX_ROW_CHUNK = 512
CONV_TILE = 32
ATTN_UNROLL = {True: 1, False: 4}
CTX_SEQS_PER_STEP = 2
ROW_SUMS_ON_MXU = {True: True, False: False}
SOFTMAX_ROWS = 32
MOD_ROWS = 16
VMEM_LIMIT_BYTES = 56 * 1024 * 1024

F32 = jnp.float32
BF16 = jnp.bfloat16


def _silu(x):
    return x * jax.nn.sigmoid(x)


def _lane_iota(shape):
    return lax.broadcasted_iota(jnp.int32, shape, 1)


def _dup_halves(x):
    r = pltpu.roll(x, HEAD_DIM, axis=1)
    lo = _lane_iota(x.shape) < HEAD_DIM
    return jnp.where(lo, x, r), jnp.where(lo, r, x)


def _with_ones(x):
    r = pltpu.roll(x, HEAD_DIM, axis=1)
    lo = _lane_iota(x.shape) < HEAD_DIM
    return jnp.where(lo, x, 1.0), jnp.where(lo, r, 1.0)


def _rope(x, cos, sin):
    quarter = HEAD_DIM // 4
    fwd = pltpu.roll(x, LANES - quarter, axis=1)
    bwd = pltpu.roll(x, quarter, axis=1)
    first = (_lane_iota(x.shape) & (2 * quarter - 1)) < quarter
    return x * cos + jnp.where(first, fwd, bwd) * sin


def _rms(x):
    return x * lax.rsqrt(jnp.mean(x * x, axis=-1, keepdims=True) + EPS)


def _adaln_kernel(c_ref, w_ref, b_ref, o_ref):
    s = _silu(c_ref[...])
    o_ref[0] = jnp.dot(s.astype(BF16), w_ref[0].astype(BF16), preferred_element_type=F32) + b_ref[0]


def _adaln(cvecs, w_ada, b_ada):
    return pl.pallas_call(
        _adaln_kernel,
        grid=(DEPTH,),
        in_specs=[
            pl.BlockSpec((MOD_ROWS, D_MODEL), lambda l: (0, 0)),
            pl.BlockSpec((1, D_MODEL, 3 * D_MODEL), lambda l: (l, 0, 0)),
            pl.BlockSpec((1, 1, 3 * D_MODEL), lambda l: (l, 0, 0)),
        ],
        out_specs=pl.BlockSpec((1, MOD_ROWS, 3 * D_MODEL), lambda l: (l, 0, 0)),
        out_shape=jax.ShapeDtypeStruct((DEPTH, MOD_ROWS, 3 * D_MODEL), F32),
        compiler_params=pltpu.CompilerParams(
            dimension_semantics=("arbitrary",),
            vmem_limit_bytes=VMEM_LIMIT_BYTES),
        name="adaln_mod",
    )(cvecs, w_ada, b_ada.reshape(DEPTH, 1, 3 * D_MODEL))


def _layers_kernel(*refs, L, n_seq, row_chunk, latent):
    if latent:
        (sink_ref, x_ref, mod_ref, nw_ref, win_ref, dw_ref, vec_ref, pw_ref, pbd_ref, wout_ref,
         fnw_ref, cos_ref, sin_ref, ck_ref, cv_ref,
         o_ref,
         gpad_ref, cpad_ref, gate_ref, qm_ref, kk_ref, vv_ref, mix_ref, s_ref, p_ref, es_ref,
         ckk_ref, cvv_ref) = refs
    else:
        (sink_ref, x_ref, mod_ref, nw_ref, win_ref, dw_ref, vec_ref, pw_ref, pbd_ref, wout_ref,
         fnw_ref,
         o_ref, nk_ref, nv_ref,
         gpad_ref, cpad_ref, gate_ref, qm_ref, kk_ref, vv_ref, mix_ref, s_ref, p_ref, es_ref) = refs

    l = pl.program_id(1)
    n_chunks = n_seq * L // row_chunk
    kpad = BLOCK if latent else 0
    assert not latent or n_seq == 1

    def locate(r0):
        return divmod(r0, L)

    @pl.when(l == 0)
    def _():
        o_ref[...] = x_ref[...]

    for c in range(CONV_W // LANES):
        for s in range(n_seq):
            gpad_ref[c, s, 0:CONV_PAD, :] = jnp.zeros((CONV_PAD, LANES), F32)
            gpad_ref[c, s, CONV_PAD + L:CONV_PAD + L + CONV_PAD, :] = jnp.zeros((CONV_PAD, LANES), F32)
            cpad_ref[c, s, 0:POOL_PAD, :] = jnp.zeros((POOL_PAD, LANES), F32)
            cpad_ref[c, s, POOL_PAD + L:POOL_PAD + L + POOL_PAD, :] = jnp.zeros((POOL_PAD, LANES), F32)
    if latent:
        for g in range(N_KV_HEADS):
            for ref in (kk_ref, vv_ref):
                ref[g, 0:kpad, :] = jnp.zeros((kpad, LANES), BF16)
                ref[g, kpad + L:kpad + L + kpad, :] = jnp.zeros((kpad, LANES), BF16)
        ck0, ck1 = _dup_halves(ck_ref[0, 0])
        cv0, cv1 = _with_ones(cv_ref[0, 0]) if ROW_SUMS_ON_MXU[latent] else _dup_halves(cv_ref[0, 0])
        ckk_ref[0] = ck0.astype(BF16)
        ckk_ref[1] = ck1.astype(BF16)
        cvv_ref[0] = cv0.astype(BF16)
        cvv_ref[1] = cv1.astype(BF16)

    shift = mod_ref[0, 0, 0:1, :]
    scl = mod_ref[0, 0, 1:2, :]
    gate = mod_ref[0, 0, 2:3, :]
    norm_scale = nw_ref[0] * (1.0 + scl)

    def proj_chunk(c):
        r0 = c * row_chunk
        seq, rs = locate(r0)
        rows = pl.ds(r0, row_chunk)
        seq_rows = pl.ds(rs, row_chunk)
        x = o_ref[seq, seq_rows, :]
        h = _rms(x) * norm_scale + shift
        hb = h.astype(BF16)

        ua = jnp.dot(hb, win_ref[0, :, OFF_A:OFF_Q], preferred_element_type=F32)
        glu = ua[:, 0:CONV_W] * jax.nn.sigmoid(ua[:, CONV_W:2 * CONV_W])
        for c in range(CONV_W // LANES):
            gpad_ref[c, seq, pl.ds(rs + CONV_PAD, row_chunk), :] = glu[:, c * LANES:(c + 1) * LANES]
        gate_ref[rows, 0:CONV_W] = _silu(ua[:, 2 * CONV_W:3 * CONV_W])

        uq = jnp.dot(hb, win_ref[0, :, OFF_Q:OFF_KV], preferred_element_type=F32)
        ukv = jnp.dot(hb, win_ref[0, :, OFF_KV:OFF_B_GATE], preferred_element_type=F32)
        k = ukv[:, 0:KV_W]
        v = ukv[:, KV_W:2 * KV_W]
        if latent:
            cos = cos_ref[seq_rows, :]
            sin = sin_ref[seq_rows, :]
            k = _rope(k, cos, sin)
        else:
            nk_ref[seq, 0, seq_rows, :] = k
            nv_ref[seq, 0, seq_rows, :] = v
        lo = _lane_iota((row_chunk, LANES)) < HEAD_DIM
        scale = HEAD_DIM ** -0.5 * LOG2E
        for p in range(ATTN_W // LANES):
            qp = uq[:, p * LANES:(p + 1) * LANES]
            if latent:
                qp = _rope(qp, cos, sin)
            qp = qp * scale
            g, c = divmod(2 * p, GQA_GROUP)
            qm_ref[g, rows, c * LANES:(c + 1) * LANES] = jnp.where(lo, qp, 0.0).astype(BF16)
            qm_ref[g, rows, (c + 1) * LANES:(c + 2) * LANES] = jnp.where(lo, 0.0, qp).astype(BF16)
        k0, k1 = _dup_halves(k)
        v0, v1 = _with_ones(v) if ROW_SUMS_ON_MXU[latent] else _dup_halves(v)
        krows = pl.ds(r0 + kpad, row_chunk)
        kk_ref[0, krows, :] = k0.astype(BF16)
        kk_ref[1, krows, :] = k1.astype(BF16)
        vv_ref[0, krows, :] = v0.astype(BF16)
        vv_ref[1, krows, :] = v1.astype(BF16)

        ub = jnp.dot(hb, win_ref[0, :, OFF_B_GATE:OFF_C], preferred_element_type=F32)
        gate_ref[rows, CONV_W:CONV_W + ATTN_W] = _silu(ub)

        uc = jnp.dot(hb, win_ref[0, :, OFF_C:IN_W], preferred_element_type=F32)
        for c in range(POOL_W // LANES):
            cpad_ref[c, seq, pl.ds(rs + POOL_PAD, row_chunk), :] = uc[:, c * LANES:(c + 1) * LANES]
        gate_ref[rows, CONV_W + ATTN_W:D_MODEL] = _silu(uc[:, POOL_W:2 * POOL_W])

    conv_b = vec_ref[0, 0:1, :]
    ln_g = vec_ref[0, 1:2, :]
    ln_b = vec_ref[0, 2:3, :]
    pool_scale = vec_ref[0, 3:4, :]

    def conv_rows(r0):
        seq, rs = locate(r0)
        rows = pl.ds(r0, BLOCK)
        pieces = []
        for t in range(BLOCK // CONV_TILE):
            base = rs + t * CONV_TILE + (CONV_PAD - CONV_K // 2)
            halves = []
            for c in range(CONV_W // LANES):
                acc = jnp.zeros((CONV_TILE, LANES), F32)
                for kk in range(CONV_K):
                    acc = acc + (gpad_ref[c, seq, pl.ds(base + kk, CONV_TILE), :]
                                 * dw_ref[0, kk:kk + 1, c * LANES:(c + 1) * LANES])
                halves.append(acc)
            y = jnp.concatenate(halves, axis=1) + conv_b
            mu = jnp.mean(y, axis=-1, keepdims=True)
            yc = y - mu
            var = jnp.mean(yc * yc, axis=-1, keepdims=True)
            yn = (yc * lax.rsqrt(var + EPS)) * ln_g + ln_b
            pieces.append(_silu(yn).astype(BF16))
        z = jnp.concatenate(pieces, axis=0)
        a = jnp.dot(z, pw_ref[0], preferred_element_type=F32)
        mix_ref[rows, 0:CONV_W] = (a * gate_ref[rows, 0:CONV_W]).astype(BF16)

    def pool_rows(r0):
        seq, rs = locate(r0)
        rows = pl.ds(r0, BLOCK)
        t = rs + lax.broadcasted_iota(jnp.int32, (BLOCK, LANES), 0)
        first_group = _lane_iota((BLOCK, LANES)) < POOL_GROUP_W
        deltas = []
        for c in range(POOL_W // LANES):

            def tok(off, c=c):
                return cpad_ref[c, seq, pl.ds(rs + POOL_PAD + off, BLOCK), :]

            centre = tok(0)
            half_a, half_b = POOL_WINDOWS[2 * c] // 2, POOL_WINDOWS[2 * c + 1] // 2
            sum_a = centre
            for off in list(range(-half_a, 0)) + list(range(1, half_a)):
                sum_a = sum_a + tok(off)
            sum_b = sum_a
            for off in list(range(-half_b, -half_a)) + list(range(half_a, half_b)):
                sum_b = sum_b + tok(off)
            half_w = jnp.where(first_group, half_a, half_b)
            cnt = jnp.minimum(t + half_w, L) - jnp.maximum(t - half_w, 0)
            mean = jnp.where(first_group, sum_a, sum_b) / cnt.astype(F32)
            deltas.append((mean - centre).astype(BF16))
        d = jnp.concatenate(deltas, axis=1)
        c_out = jnp.dot(d, pbd_ref[0], preferred_element_type=F32) * pool_scale
        c0 = CONV_W + ATTN_W
        mix_ref[rows, c0:D_MODEL] = (c_out * gate_ref[rows, c0:D_MODEL]).astype(BF16)

    n_blocks = n_seq * L // BLOCK
    seq_blocks = L // BLOCK
    n_local = 3 * BLOCK if latent else L
    lo_q = _lane_iota((BLOCK, LANES)) < HEAD_DIM
    contract_last = (((1,), (1,)), ((), ()))

    def fold(pieces, op):
        acc = None
        for piece in pieces:
            for c in range(piece.shape[1] // LANES):
                slab = piece[:, c * LANES:(c + 1) * LANES]
                acc = slab if acc is None else op(acc, slab)
        return acc

    def block_rows(i):
        if isinstance(i, int):
            q0, k0 = i * BLOCK, (i // seq_blocks) * L
        else:
            q0 = pl.multiple_of(i * BLOCK, BLOCK)
            k0 = pl.multiple_of(lax.div(i, jnp.int32(seq_blocks)) * L, L)
        return q0, (pl.ds(q0, n_local) if latent else pl.ds(k0, n_local))

    def scores(i, g):
        q0, local_rows = block_rows(i)
        qm = jnp.concatenate([qm_ref[g, pl.ds(q0, BLOCK), j * LANES:(j + 1) * LANES]
                              for j in range(GQA_GROUP)], axis=0)
        s_ref[g, :, 0:n_local] = lax.dot_general(qm, kk_ref[g, local_rows, :], contract_last,
                                                 preferred_element_type=F32)
        if latent:
            s_ref[g, :, n_local:] = lax.dot_general(qm, ckk_ref[g], contract_last,
                                                    preferred_element_type=F32)

    def softmax_pv(i, g):
        q0, local_rows = block_rows(i)
        qrows = pl.ds(q0, BLOCK)
        if latent:
            jq = lax.broadcasted_iota(jnp.int32, (SOFTMAX_ROWS, BLOCK), 0)
            jk = lax.broadcasted_iota(jnp.int32, (SOFTMAX_ROWS, BLOCK), 1)
            prev_off = jnp.where(i > 0, 0, BLOCK)
            next_off = jnp.where(i < n_blocks - 1, 0, BLOCK)
        for j in range(GQA_GROUP):
            sink = sink_ref[l, g * GQA_GROUP + j] * LOG2E
            for r in range(BLOCK // SOFTMAX_ROWS):
                rr = slice(j * BLOCK + r * SOFTMAX_ROWS, j * BLOCK + (r + 1) * SOFTMAX_ROWS)
                sj = s_ref[g, rr, :]
                if latent:
                    prev_ok = jk >= jq + (prev_off + r * SOFTMAX_ROWS)
                    next_ok = jk <= jq + (r * SOFTMAX_ROWS - next_off)
                    pieces = [jnp.where(prev_ok, sj[:, 0:BLOCK], NEG_INF),
                              sj[:, BLOCK:2 * BLOCK],
                              jnp.where(next_ok, sj[:, 2 * BLOCK:3 * BLOCK], NEG_INF),
                              sj[:, n_local:]]
                else:
                    pieces = [sj]
                m = jnp.maximum(jnp.max(fold(pieces, jnp.maximum), axis=-1, keepdims=True), sink)
                e = [jnp.exp2(piece - m) for piece in pieces]
                p_ref[g, rr, :] = jnp.concatenate(e, axis=1).astype(BF16)
                tail = jnp.exp2(sink - m)
                if not ROW_SUMS_ON_MXU[latent]:
                    tail = tail + jnp.sum(fold(e, jnp.add), axis=-1, keepdims=True)
                es_ref[g, rr, :] = jnp.broadcast_to(tail, (SOFTMAX_ROWS, LANES))
        o = jnp.dot(p_ref[g, :, 0:n_local], vv_ref[g, local_rows, :],
                    preferred_element_type=F32)
        if latent:
            o = o + jnp.dot(p_ref[g, :, n_local:], cvv_ref[g], preferred_element_type=F32)
        pairs = []
        for p in range(GQA_GROUP // 2):
            rows_a = slice((2 * p) * BLOCK, (2 * p + 1) * BLOCK)
            rows_b = slice((2 * p + 1) * BLOCK, (2 * p + 2) * BLOCK)
            o_a = o[rows_a]
            o_b = o[rows_b]
            if ROW_SUMS_ON_MXU[latent]:
                inv_a = 1.0 / (o_a + es_ref[g, rows_a, :])
                inv_b = 1.0 / (o_b + es_ref[g, rows_b, :])
                pairs.append(jnp.where(lo_q, o_a * pltpu.roll(inv_a, HEAD_DIM, axis=1),
                                       pltpu.roll(o_b, HEAD_DIM, axis=1) * inv_b))
            else:
                pairs.append(jnp.where(lo_q, o_a / es_ref[g, rows_a, :], o_b / es_ref[g, rows_b, :]))
        c0 = CONV_W + g * (ATTN_W // N_KV_HEADS)
        cols = slice(c0, c0 + ATTN_W // N_KV_HEADS)
        mix_ref[qrows, cols] = (jnp.concatenate(pairs, axis=1) * gate_ref[qrows, cols]).astype(BF16)

    def attn_block(i, carry):
        scores(i, 1)
        softmax_pv(i, 0)
        scores(jnp.minimum(i + 1, n_blocks - 1), 0)
        softmax_pv(i, 1)
        return carry

    def out_chunk(c):
        seq, rs = locate(c * row_chunk)
        seq_rows = pl.ds(rs, row_chunk)
        y = jnp.dot(mix_ref[pl.ds(c * row_chunk, row_chunk), :], wout_ref[0], preferred_element_type=F32)
        o_ref[seq, seq_rows, :] = o_ref[seq, seq_rows, :] + gate * y

    def vpu_branches(c):
        for b in range(row_chunk // BLOCK):
            conv_rows(c * row_chunk + b * BLOCK)
            pool_rows(c * row_chunk + b * BLOCK)

    for c in range(n_chunks):
        proj_chunk(c)
        if c > 0:
            vpu_branches(c - 1)
    scores(0, 0)
    lax.fori_loop(0, n_blocks, attn_block, 0, unroll=min(n_blocks, ATTN_UNROLL[latent]))
    last_blocks = [(n_chunks - 1) * row_chunk + b * BLOCK for b in range(row_chunk // BLOCK)]
    for c in range(n_chunks - 1):
        out_chunk(c)
        if last_blocks:
            r0 = last_blocks.pop(0)
            conv_rows(r0)
            pool_rows(r0)
    for r0 in last_blocks:
        conv_rows(r0)
        pool_rows(r0)
    out_chunk(n_chunks - 1)

    @pl.when(l == DEPTH - 1)
    def _():
        for c in range(n_chunks):
            seq, rs = locate(c * row_chunk)
            seq_rows = pl.ds(rs, row_chunk)
            o_ref[seq, seq_rows, :] = _rms(o_ref[seq, seq_rows, :]) * fnw_ref[...]


def _mixer_layers(x, mods, mod_row, weights, n_seq, rope=None, cache=None):
    S, L, _ = x.shape
    latent = cache is not None
    sink, norm_w, w_in, conv_dw, vecs, conv_pw, pool_bd, w_out, final_norm_w = weights
    kpad = BLOCK if latent else 0
    n_keys = 3 * BLOCK + cache[0].shape[2] if latent else L
    R = n_seq * L
    row_chunk = min(MAX_ROW_CHUNK, L)
    assert S % n_seq == 0 and L % row_chunk == 0 and row_chunk % BLOCK == 0

    per_layer = lambda *tail: (lambda s, l: (l,) + tail)
    in_specs = [
        pl.BlockSpec(memory_space=pltpu.SMEM),
        pl.BlockSpec((n_seq, L, D_MODEL), lambda s, l: (s, 0, 0)),
        pl.BlockSpec((1, 1, 3, D_MODEL), lambda s, l: (l, mod_row(s), 0, 0)),
        pl.BlockSpec((1, 1, D_MODEL), per_layer(0, 0)),
        pl.BlockSpec((1, D_MODEL, IN_W), per_layer(0, 0)),
        pl.BlockSpec((1, CONV_K_PADDED, CONV_W), per_layer(0, 0)),
        pl.BlockSpec((1, 8, CONV_W), per_layer(0, 0)),
        pl.BlockSpec((1, CONV_W, CONV_W), per_layer(0, 0)),
        pl.BlockSpec((1, POOL_W, POOL_W), per_layer(0, 0)),
        pl.BlockSpec((1, D_MODEL, D_MODEL), per_layer(0, 0)),
        pl.BlockSpec((1, D_MODEL), lambda s, l: (0, 0)),
    ]
    args = [sink, x, mods, norm_w, w_in, conv_dw, vecs, conv_pw, pool_bd, w_out, final_norm_w]
    out_specs = [pl.BlockSpec((n_seq, L, D_MODEL), lambda s, l: (s, 0, 0))]
    out_shape = [jax.ShapeDtypeStruct((S, L, D_MODEL), F32)]
    scratch = [
        pltpu.VMEM((CONV_W // LANES, n_seq, L + 2 * CONV_PAD, LANES), F32),
        pltpu.VMEM((POOL_W // LANES, n_seq, L + 2 * POOL_PAD, LANES), F32),
        pltpu.VMEM((R, D_MODEL), F32),
        pltpu.VMEM((N_KV_HEADS, R, ATTN_W), BF16),
        pltpu.VMEM((N_KV_HEADS, R + 2 * kpad, LANES), BF16),
        pltpu.VMEM((N_KV_HEADS, R + 2 * kpad, LANES), BF16),
        pltpu.VMEM((R, D_MODEL), BF16),
        pltpu.VMEM((N_KV_HEADS, GQA_GROUP * BLOCK, n_keys), F32),
        pltpu.VMEM((N_KV_HEADS, GQA_GROUP * BLOCK, n_keys), BF16),
        pltpu.VMEM((N_KV_HEADS, GQA_GROUP * BLOCK, LANES), F32),
    ]
    if latent:
        cos, sin = rope
        ck, cv = cache
        past = ck.shape[2]
        in_specs += [
            pl.BlockSpec((L, LANES), lambda s, l: (0, 0)),
            pl.BlockSpec((L, LANES), lambda s, l: (0, 0)),
            pl.BlockSpec((1, 1, past, KV_W), lambda s, l: (s, l, 0, 0)),
            pl.BlockSpec((1, 1, past, KV_W), lambda s, l: (s, l, 0, 0)),
        ]
        args += [cos, sin, ck, cv]
        scratch += [pltpu.VMEM((N_KV_HEADS, past, LANES), BF16),
                    pltpu.VMEM((N_KV_HEADS, past, LANES), BF16)]
    else:
        out_specs += [pl.BlockSpec((n_seq, 1, L, KV_W), lambda s, l: (s, l, 0, 0))] * 2
        out_shape += [jax.ShapeDtypeStruct((S, DEPTH, L, KV_W), F32)] * 2

    return pl.pallas_call(
        functools.partial(_layers_kernel, L=L, n_seq=n_seq, row_chunk=row_chunk, latent=latent),
        grid=(S // n_seq, DEPTH),
        in_specs=in_specs,
        out_specs=out_specs,
        out_shape=out_shape,
        scratch_shapes=scratch,
        compiler_params=pltpu.CompilerParams(
            dimension_semantics=("arbitrary", "arbitrary"),
            vmem_limit_bytes=VMEM_LIMIT_BYTES),
        name="latent_layers" if latent else "context_layers",
    )(*args)


def _rope_tables(L):
    quarter = HEAD_DIM // 4
    t = jnp.arange(L)
    rows = (t // GRID_W).astype(F32)
    cols = (t % GRID_W).astype(F32)
    freqs = ROPE_BASE ** (-jnp.arange(quarter, dtype=F32) / quarter)
    ang_r = rows[:, None] * freqs[None, :]
    ang_c = cols[:, None] * freqs[None, :]
    cos = jnp.concatenate([jnp.cos(ang_r)] * 2 + [jnp.cos(ang_c)] * 2, axis=-1)
    sin = jnp.concatenate([-jnp.sin(ang_r), jnp.sin(ang_r), -jnp.sin(ang_c), jnp.sin(ang_c)], axis=-1)
    return jnp.tile(cos, (1, 2)), jnp.tile(sin, (1, 2))


def kernel(x_prompt, x_sample, c, cache_k, cache_v, c_ctx, w_ada, b_ada, norm_w, w_in, conv_dw, conv_b,
           conv_ln_g, conv_ln_b, conv_pw, attn_sink, pool_w, pool_scale, w_out, final_norm_w):
    n_ctx, ctx_len, _ = x_prompt.shape
    n_lat, lat_len, _ = x_sample.shape
    past = cache_k.shape[2]
    assert n_lat + 1 <= MOD_ROWS

    cvecs = jnp.zeros((MOD_ROWS, D_MODEL), F32).at[:n_lat].set(c).at[n_lat].set(c_ctx)
    mods = _adaln(cvecs, w_ada, b_ada).reshape(DEPTH, MOD_ROWS, 3, D_MODEL)

    n_groups = len(POOL_WINDOWS)
    eye = jnp.eye(n_groups, dtype=F32)
    pool_bd = (pool_w[:, :, :, None, :] * eye[None, :, None, :, None]).reshape(DEPTH, POOL_W, POOL_W)
    vecs = jnp.zeros((DEPTH, 8, CONV_W), F32)
    vecs = vecs.at[:, 0].set(conv_b).at[:, 1].set(conv_ln_g).at[:, 2].set(conv_ln_b).at[:, 3].set(pool_scale)
    weights = (
        attn_sink,
        norm_w.reshape(DEPTH, 1, D_MODEL),
        w_in.astype(BF16),
        jnp.pad(conv_dw, ((0, 0), (0, CONV_K_PADDED - CONV_K), (0, 0))),
        vecs,
        conv_pw.astype(BF16),
        pool_bd.astype(BF16),
        w_out.astype(BF16),
        final_norm_w.reshape(1, D_MODEL),
    )

    n_seq_ctx = max(d for d in range(1, CTX_SEQS_PER_STEP + 1) if n_ctx % d == 0)
    y_prompt, new_k, new_v = _mixer_layers(x_prompt, mods, lambda s: n_lat, weights, n_seq_ctx)
    new_cache_k = new_k.reshape(n_ctx, DEPTH, ctx_len, N_KV_HEADS, HEAD_DIM)
    new_cache_v = new_v.reshape(n_ctx, DEPTH, ctx_len, N_KV_HEADS, HEAD_DIM)

    (y_sample,) = _mixer_layers(
        x_sample, mods, lambda s: s, weights, 1,
        rope=_rope_tables(lat_len),
        cache=(cache_k.reshape(n_lat, DEPTH, past, KV_W), cache_v.reshape(n_lat, DEPTH, past, KV_W)))
    return (y_prompt, y_sample, new_cache_k, new_cache_v)
```

```python
import functools

import jax
import jax.numpy as jnp
from jax import lax
from jax.experimental import pallas as pl
from jax.experimental.pallas import tpu as pltpu

D_MODEL = 1024
DEPTH = 4
GRID_W = 64
CONV_W = 256
ATTN_W = 512
POOL_W = 256
HEAD_DIM = 64
N_HEADS = 8
N_KV_HEADS = 2
GQA_GROUP = 4
KV_W = 128
WINDOW = 128
BLOCK = 128
CONV_K = 31
POOL_WINDOWS = (2, 4, 8, 16)
POOL_GROUP_W = 64
ROPE_BASE = 10000.0
EPS = 1e-6
NEG_INF = -1e30
LOG2E = 1.4426950408889634

OFF_A = 0
OFF_Q = 3 * CONV_W
OFF_KV = OFF_Q + ATTN_W
OFF_B_GATE = OFF_KV + 2 * KV_W
OFF_C = OFF_B_GATE + ATTN_W
IN_W = OFF_C + 2 * POOL_W

LANES = 128
CONV_K_PADDED = 32
CONV_PAD = 16
POOL_PAD = 8
MAX_ROW_CHUNK = 512
CONV_TILE = 32
ATTN_UNROLL = {True: 4, False: 4}
CTX_SEQS_PER_STEP = 2
ROW_SUMS_ON_MXU = {True: True, False: False}
SOFTMAX_ROWS = 32
MOD_ROWS = 16
VMEM_LIMIT_BYTES = 56 * 1024 * 1024

F32 = jnp.float32
BF16 = jnp.bfloat16


def _silu(x):
    return x * jax.nn.sigmoid(x)


def _lane_iota(shape):
    return lax.broadcasted_iota(jnp.int32, shape, 1)


def _dup_halves(x):
    r = pltpu.roll(x, HEAD_DIM, axis=1)
    lo = _lane_iota(x.shape) < HEAD_DIM
    return jnp.where(lo, x, r), jnp.where(lo, r, x)


def _with_ones(x):
    r = pltpu.roll(x, HEAD_DIM, axis=1)
    lo = _lane_iota(x.shape) < HEAD_DIM
    return jnp.where(lo, x, 1.0), jnp.where(lo, r, 1.0)


def _rope(x, cos, sin):
    quarter = HEAD_DIM // 4
    fwd = pltpu.roll(x, LANES - quarter, axis=1)
    bwd = pltpu.roll(x, quarter, axis=1)
    first = (_lane_iota(x.shape) & (2 * quarter - 1)) < quarter
    return x * cos + jnp.where(first, fwd, bwd) * sin


def _rms(x):
    return x * lax.rsqrt(jnp.mean(x * x, axis=-1, keepdims=True) + EPS)


def _adaln_kernel(c_ref, w_ref, b_ref, o_ref):
    s = _silu(c_ref[...])
    o_ref[0] = jnp.dot(s.astype(BF16), w_ref[0].astype(BF16), preferred_element_type=F32) + b_ref[0]


def _adaln(cvecs, w_ada, b_ada):
    return pl.pallas_call(
        _adaln_kernel,
        grid=(DEPTH,),
        in_specs=[
            pl.BlockSpec((MOD_ROWS, D_MODEL), lambda l: (0, 0)),
            pl.BlockSpec((1, D_MODEL, 3 * D_MODEL), lambda l: (l, 0, 0)),
            pl.BlockSpec((1, 1, 3 * D_MODEL), lambda l: (l, 0, 0)),
        ],
        out_specs=pl.BlockSpec((1, MOD_ROWS, 3 * D_MODEL), lambda l: (l, 0, 0)),
        out_shape=jax.ShapeDtypeStruct((DEPTH, MOD_ROWS, 3 * D_MODEL), F32),
        compiler_params=pltpu.CompilerParams(
            dimension_semantics=("arbitrary",),
            vmem_limit_bytes=VMEM_LIMIT_BYTES),
        name="adaln_mod",
    )(cvecs, w_ada, b_ada.reshape(DEPTH, 1, 3 * D_MODEL))


def _layers_kernel(*refs, L, n_seq, row_chunk, latent):
    if latent:
        (sink_ref, x_ref, mod_ref, nw_ref, win_ref, dw_ref, vec_ref, pw_ref, pbd_ref, wout_ref,
         fnw_ref, cos_ref, sin_ref, ck_ref, cv_ref,
         o_ref,
         gpad_ref, cpad_ref, gate_ref, qm_ref, kk_ref, vv_ref, mix_ref, s_ref, p_ref, es_ref,
         ckk_ref, cvv_ref) = refs
    else:
        (sink_ref, x_ref, mod_ref, nw_ref, win_ref, dw_ref, vec_ref, pw_ref, pbd_ref, wout_ref,
         fnw_ref,
         o_ref, nk_ref, nv_ref,
         gpad_ref, cpad_ref, gate_ref, qm_ref, kk_ref, vv_ref, mix_ref, s_ref, p_ref, es_ref) = refs

    l = pl.program_id(1)
    n_chunks = n_seq * L // row_chunk
    kpad = BLOCK if latent else 0
    assert not latent or n_seq == 1

    def locate(r0):
        return divmod(r0, L)

    @pl.when(l == 0)
    def _():
        o_ref[...] = x_ref[...]

    for c in range(CONV_W // LANES):
        for s in range(n_seq):
            gpad_ref[c, s, 0:CONV_PAD, :] = jnp.zeros((CONV_PAD, LANES), F32)
            gpad_ref[c, s, CONV_PAD + L:CONV_PAD + L + CONV_PAD, :] = jnp.zeros((CONV_PAD, LANES), F32)
            cpad_ref[c, s, 0:POOL_PAD, :] = jnp.zeros((POOL_PAD, LANES), F32)
            cpad_ref[c, s, POOL_PAD + L:POOL_PAD + L + POOL_PAD, :] = jnp.zeros((POOL_PAD, LANES), F32)
    if latent:
        for g in range(N_KV_HEADS):
            for ref in (kk_ref, vv_ref):
                ref[g, 0:kpad, :] = jnp.zeros((kpad, LANES), BF16)
                ref[g, kpad + L:kpad + L + kpad, :] = jnp.zeros((kpad, LANES), BF16)
        ck0, ck1 = _dup_halves(ck_ref[0, 0])
        cv0, cv1 = _with_ones(cv_ref[0, 0]) if ROW_SUMS_ON_MXU[latent] else _dup_halves(cv_ref[0, 0])
        ckk_ref[0] = ck0.astype(BF16)
        ckk_ref[1] = ck1.astype(BF16)
        cvv_ref[0] = cv0.astype(BF16)
        cvv_ref[1] = cv1.astype(BF16)

    shift = mod_ref[0, 0, 0:1, :]
    scl = mod_ref[0, 0, 1:2, :]
    gate = mod_ref[0, 0, 2:3, :]
    norm_scale = nw_ref[0] * (1.0 + scl)

    def proj_chunk(c):
        r0 = c * row_chunk
        seq, rs = locate(r0)
        rows = pl.ds(r0, row_chunk)
        seq_rows = pl.ds(rs, row_chunk)
        x = o_ref[seq, seq_rows, :]
        h = _rms(x) * norm_scale + shift
        hb = h.astype(BF16)

        ua = jnp.dot(hb, win_ref[0, :, OFF_A:OFF_Q], preferred_element_type=F32)
        glu = ua[:, 0:CONV_W] * jax.nn.sigmoid(ua[:, CONV_W:2 * CONV_W])
        for c in range(CONV_W // LANES):
            gpad_ref[c, seq, pl.ds(rs + CONV_PAD, row_chunk), :] = glu[:, c * LANES:(c + 1) * LANES]
        gate_ref[rows, 0:CONV_W] = _silu(ua[:, 2 * CONV_W:3 * CONV_W])

        uq = jnp.dot(hb, win_ref[0, :, OFF_Q:OFF_KV], preferred_element_type=F32)
        ukv = jnp.dot(hb, win_ref[0, :, OFF_KV:OFF_B_GATE], preferred_element_type=F32)
        k = ukv[:, 0:KV_W]
        v = ukv[:, KV_W:2 * KV_W]
        if latent:
            cos = cos_ref[seq_rows, :]
            sin = sin_ref[seq_rows, :]
            k = _rope(k, cos, sin)
        else:
            nk_ref[seq, 0, seq_rows, :] = k
            nv_ref[seq, 0, seq_rows, :] = v
        lo = _lane_iota((row_chunk, LANES)) < HEAD_DIM
        scale = HEAD_DIM ** -0.5 * LOG2E
        for p in range(ATTN_W // LANES):
            qp = uq[:, p * LANES:(p + 1) * LANES]
            if latent:
                qp = _rope(qp, cos, sin)
            qp = qp * scale
            g, c = divmod(2 * p, GQA_GROUP)
            qm_ref[g, rows, c * LANES:(c + 1) * LANES] = jnp.where(lo, qp, 0.0).astype(BF16)
            qm_ref[g, rows, (c + 1) * LANES:(c + 2) * LANES] = jnp.where(lo, 0.0, qp).astype(BF16)
        k0, k1 = _dup_halves(k)
        v0, v1 = _with_ones(v) if ROW_SUMS_ON_MXU[latent] else _dup_halves(v)
        krows = pl.ds(r0 + kpad, row_chunk)
        kk_ref[0, krows, :] = k0.astype(BF16)
        kk_ref[1, krows, :] = k1.astype(BF16)
        vv_ref[0, krows, :] = v0.astype(BF16)
        vv_ref[1, krows, :] = v1.astype(BF16)

        ub = jnp.dot(hb, win_ref[0, :, OFF_B_GATE:OFF_C], preferred_element_type=F32)
        gate_ref[rows, CONV_W:CONV_W + ATTN_W] = _silu(ub)

        uc = jnp.dot(hb, win_ref[0, :, OFF_C:IN_W], preferred_element_type=F32)
        for c in range(POOL_W // LANES):
            cpad_ref[c, seq, pl.ds(rs + POOL_PAD, row_chunk), :] = uc[:, c * LANES:(c + 1) * LANES]
        gate_ref[rows, CONV_W + ATTN_W:D_MODEL] = _silu(uc[:, POOL_W:2 * POOL_W])

    conv_b = vec_ref[0, 0:1, :]
    ln_g = vec_ref[0, 1:2, :]
    ln_b = vec_ref[0, 2:3, :]
    pool_scale = vec_ref[0, 3:4, :]

    def conv_rows(r0):
        seq, rs = locate(r0)
        rows = pl.ds(r0, BLOCK)
        pieces = []
        for t in range(BLOCK // CONV_TILE):
            base = rs + t * CONV_TILE + (CONV_PAD - CONV_K // 2)
            halves = []
            for c in range(CONV_W // LANES):
                acc = jnp.zeros((CONV_TILE, LANES), F32)
                for kk in range(CONV_K):
                    acc = acc + (gpad_ref[c, seq, pl.ds(base + kk, CONV_TILE), :]
                                 * dw_ref[0, kk:kk + 1, c * LANES:(c + 1) * LANES])
                halves.append(acc)
            y = jnp.concatenate(halves, axis=1) + conv_b
            mu = jnp.mean(y, axis=-1, keepdims=True)
            yc = y - mu
            var = jnp.mean(yc * yc, axis=-1, keepdims=True)
            yn = (yc * lax.rsqrt(var + EPS)) * ln_g + ln_b
            pieces.append(_silu(yn).astype(BF16))
        z = jnp.concatenate(pieces, axis=0)
        a = jnp.dot(z, pw_ref[0], preferred_element_type=F32)
        mix_ref[rows, 0:CONV_W] = (a * gate_ref[rows, 0:CONV_W]).astype(BF16)

    def pool_rows(r0):
        seq, rs = locate(r0)
        rows = pl.ds(r0, BLOCK)
        t = rs + lax.broadcasted_iota(jnp.int32, (BLOCK, LANES), 0)
        first_group = _lane_iota((BLOCK, LANES)) < POOL_GROUP_W
        deltas = []
        for c in range(POOL_W // LANES):

            def tok(off, c=c):
                return cpad_ref[c, seq, pl.ds(rs + POOL_PAD + off, BLOCK), :]

            centre = tok(0)
            half_a, half_b = POOL_WINDOWS[2 * c] // 2, POOL_WINDOWS[2 * c + 1] // 2
            sum_a = centre
            for off in list(range(-half_a, 0)) + list(range(1, half_a)):
                sum_a = sum_a + tok(off)
            sum_b = sum_a
            for off in list(range(-half_b, -half_a)) + list(range(half_a, half_b)):
                sum_b = sum_b + tok(off)
            half_w = jnp.where(first_group, half_a, half_b)
            cnt = jnp.minimum(t + half_w, L) - jnp.maximum(t - half_w, 0)
            mean = jnp.where(first_group, sum_a, sum_b) / cnt.astype(F32)
            deltas.append((mean - centre).astype(BF16))
        d = jnp.concatenate(deltas, axis=1)
        c_out = jnp.dot(d, pbd_ref[0], preferred_element_type=F32) * pool_scale
        c0 = CONV_W + ATTN_W
        mix_ref[rows, c0:D_MODEL] = (c_out * gate_ref[rows, c0:D_MODEL]).astype(BF16)

    n_blocks = n_seq * L // BLOCK
    seq_blocks = L // BLOCK
    n_local = 3 * BLOCK if latent else L
    lo_q = _lane_iota((BLOCK, LANES)) < HEAD_DIM
    contract_last = (((1,), (1,)), ((), ()))

    def fold(pieces, op):
        acc = None
        for piece in pieces:
            for c in range(piece.shape[1] // LANES):
                slab = piece[:, c * LANES:(c + 1) * LANES]
                acc = slab if acc is None else op(acc, slab)
        return acc

    def block_rows(i):
        if isinstance(i, int):
            q0, k0 = i * BLOCK, (i // seq_blocks) * L
        else:
            q0 = pl.multiple_of(i * BLOCK, BLOCK)
            k0 = pl.multiple_of(lax.div(i, jnp.int32(seq_blocks)) * L, L)
        return q0, (pl.ds(q0, n_local) if latent else pl.ds(k0, n_local))

    def scores(i, g):
        q0, local_rows = block_rows(i)
        qm = jnp.concatenate([qm_ref[g, pl.ds(q0, BLOCK), j * LANES:(j + 1) * LANES]
                              for j in range(GQA_GROUP)], axis=0)
        s_ref[g, :, 0:n_local] = lax.dot_general(qm, kk_ref[g, local_rows, :], contract_last,
                                                 preferred_element_type=F32)
        if latent:
            s_ref[g, :, n_local:] = lax.dot_general(qm, ckk_ref[g], contract_last,
                                                    preferred_element_type=F32)

    def softmax_pv(i, g):
        q0, local_rows = block_rows(i)
        qrows = pl.ds(q0, BLOCK)
        if latent:
            jq = lax.broadcasted_iota(jnp.int32, (SOFTMAX_ROWS, BLOCK), 0)
            jk = lax.broadcasted_iota(jnp.int32, (SOFTMAX_ROWS, BLOCK), 1)
            prev_off = jnp.where(i > 0, 0, BLOCK)
            next_off = jnp.where(i < n_blocks - 1, 0, BLOCK)
        for j in range(GQA_GROUP):
            sink = sink_ref[l, g * GQA_GROUP + j] * LOG2E
            for r in range(BLOCK // SOFTMAX_ROWS):
                rr = slice(j * BLOCK + r * SOFTMAX_ROWS, j * BLOCK + (r + 1) * SOFTMAX_ROWS)
                sj = s_ref[g, rr, :]
                if latent:
                    prev_ok = jk >= jq + (prev_off + r * SOFTMAX_ROWS)
                    next_ok = jk <= jq + (r * SOFTMAX_ROWS - next_off)
                    pieces = [jnp.where(prev_ok, sj[:, 0:BLOCK], NEG_INF),
                              sj[:, BLOCK:2 * BLOCK],
                              jnp.where(next_ok, sj[:, 2 * BLOCK:3 * BLOCK], NEG_INF),
                              sj[:, n_local:]]
                else:
                    pieces = [sj]
                m = jnp.maximum(jnp.max(fold(pieces, jnp.maximum), axis=-1, keepdims=True), sink)
                e = [jnp.exp2(piece - m) for piece in pieces]
                p_ref[g, rr, :] = jnp.concatenate(e, axis=1).astype(BF16)
                tail = jnp.exp2(sink - m)
                if not ROW_SUMS_ON_MXU[latent]:
                    tail = tail + jnp.sum(fold(e, jnp.add), axis=-1, keepdims=True)
                es_ref[g, rr, :] = jnp.broadcast_to(tail, (SOFTMAX_ROWS, LANES))
        o = jnp.dot(p_ref[g, :, 0:n_local], vv_ref[g, local_rows, :],
                    preferred_element_type=F32)
        if latent:
            o = o + jnp.dot(p_ref[g, :, n_local:], cvv_ref[g], preferred_element_type=F32)
        pairs = []
        for p in range(GQA_GROUP // 2):
            rows_a = slice((2 * p) * BLOCK, (2 * p + 1) * BLOCK)
            rows_b = slice((2 * p + 1) * BLOCK, (2 * p + 2) * BLOCK)
            o_a = o[rows_a]
            o_b = o[rows_b]
            if ROW_SUMS_ON_MXU[latent]:
                inv_a = 1.0 / (o_a + es_ref[g, rows_a, :])
                inv_b = 1.0 / (o_b + es_ref[g, rows_b, :])
                pairs.append(jnp.where(lo_q, o_a * pltpu.roll(inv_a, HEAD_DIM, axis=1),
                                       pltpu.roll(o_b, HEAD_DIM, axis=1) * inv_b))
            else:
                pairs.append(jnp.where(lo_q, o_a / es_ref[g, rows_a, :], o_b / es_ref[g, rows_b, :]))
        c0 = CONV_W + g * (ATTN_W // N_KV_HEADS)
        cols = slice(c0, c0 + ATTN_W // N_KV_HEADS)
        mix_ref[qrows, cols] = (jnp.concatenate(pairs, axis=1) * gate_ref[qrows, cols]).astype(BF16)

    def attn_block(i, carry):
        scores(i, 1)
        softmax_pv(i, 0)
        scores(jnp.minimum(i + 1, n_blocks - 1), 0)
        softmax_pv(i, 1)
        return carry

    def out_chunk(c):
        seq, rs = locate(c * row_chunk)
        seq_rows = pl.ds(rs, row_chunk)
        y = jnp.dot(mix_ref[pl.ds(c * row_chunk, row_chunk), :], wout_ref[0], preferred_element_type=F32)
        o_ref[seq, seq_rows, :] = o_ref[seq, seq_rows, :] + gate * y

    def vpu_branches(c):
        for b in range(row_chunk // BLOCK):
            conv_rows(c * row_chunk + b * BLOCK)
            pool_rows(c * row_chunk + b * BLOCK)

    for c in range(n_chunks):
        proj_chunk(c)
        if c > 0:
            vpu_branches(c - 1)
    scores(0, 0)
    lax.fori_loop(0, n_blocks, attn_block, 0, unroll=min(n_blocks, ATTN_UNROLL[latent]))
    last_blocks = [(n_chunks - 1) * row_chunk + b * BLOCK for b in range(row_chunk // BLOCK)]
    for c in range(n_chunks - 1):
        out_chunk(c)
        if last_blocks:
            r0 = last_blocks.pop(0)
            conv_rows(r0)
            pool_rows(r0)
    for r0 in last_blocks:
        conv_rows(r0)
        pool_rows(r0)
    out_chunk(n_chunks - 1)

    @pl.when(l == DEPTH - 1)
    def _():
        for c in range(n_chunks):
            seq, rs = locate(c * row_chunk)
            seq_rows = pl.ds(rs, row_chunk)
            o_ref[seq, seq_rows, :] = _rms(o_ref[seq, seq_rows, :]) * fnw_ref[...]


def _mixer_layers(x, mods, mod_row, weights, n_seq, rope=None, cache=None):
    S, L, _ = x.shape
    latent = cache is not None
    sink, norm_w, w_in, conv_dw, vecs, conv_pw, pool_bd, w_out, final_norm_w = weights
    kpad = BLOCK if latent else 0
    n_keys = 3 * BLOCK + cache[0].shape[2] if latent else L
    R = n_seq * L
    row_chunk = min(MAX_ROW_CHUNK, L)
    assert S % n_seq == 0 and L % row_chunk == 0 and row_chunk % BLOCK == 0

    per_layer = lambda *tail: (lambda s, l: (l,) + tail)
    in_specs = [
        pl.BlockSpec(memory_space=pltpu.SMEM),
        pl.BlockSpec((n_seq, L, D_MODEL), lambda s, l: (s, 0, 0),
                     pipeline_mode=pl.Buffered(1) if latent else None),
        pl.BlockSpec((1, 1, 3, D_MODEL), lambda s, l: (l, mod_row(s), 0, 0)),
        pl.BlockSpec((1, 1, D_MODEL), per_layer(0, 0)),
        pl.BlockSpec((1, D_MODEL, IN_W), per_layer(0, 0)),
        pl.BlockSpec((1, CONV_K_PADDED, CONV_W), per_layer(0, 0)),
        pl.BlockSpec((1, 8, CONV_W), per_layer(0, 0)),
        pl.BlockSpec((1, CONV_W, CONV_W), per_layer(0, 0)),
        pl.BlockSpec((1, POOL_W, POOL_W), per_layer(0, 0)),
        pl.BlockSpec((1, D_MODEL, D_MODEL), per_layer(0, 0)),
        pl.BlockSpec((1, D_MODEL), lambda s, l: (0, 0)),
    ]
    args = [sink, x, mods, norm_w, w_in, conv_dw, vecs, conv_pw, pool_bd, w_out, final_norm_w]
    out_specs = [pl.BlockSpec((n_seq, L, D_MODEL), lambda s, l: (s, 0, 0))]
    out_shape = [jax.ShapeDtypeStruct((S, L, D_MODEL), F32)]
    scratch = [
        pltpu.VMEM((CONV_W // LANES, n_seq, L + 2 * CONV_PAD, LANES), F32),
        pltpu.VMEM((POOL_W // LANES, n_seq, L + 2 * POOL_PAD, LANES), F32),
        pltpu.VMEM((R, D_MODEL), F32),
        pltpu.VMEM((N_KV_HEADS, R, ATTN_W), BF16),
        pltpu.VMEM((N_KV_HEADS, R + 2 * kpad, LANES), BF16),
        pltpu.VMEM((N_KV_HEADS, R + 2 * kpad, LANES), BF16),
        pltpu.VMEM((R, D_MODEL), BF16),
        pltpu.VMEM((N_KV_HEADS, GQA_GROUP * BLOCK, n_keys), F32),
        pltpu.VMEM((N_KV_HEADS, GQA_GROUP * BLOCK, n_keys), BF16),
        pltpu.VMEM((N_KV_HEADS, GQA_GROUP * BLOCK, LANES), F32),
    ]
    if latent:
        cos, sin = rope
        ck, cv = cache
        past = ck.shape[2]
        in_specs += [
            pl.BlockSpec((L, LANES), lambda s, l: (0, 0)),
            pl.BlockSpec((L, LANES), lambda s, l: (0, 0)),
            pl.BlockSpec((1, 1, past, KV_W), lambda s, l: (s, l, 0, 0)),
            pl.BlockSpec((1, 1, past, KV_W), lambda s, l: (s, l, 0, 0)),
        ]
        args += [cos, sin, ck, cv]
        scratch += [pltpu.VMEM((N_KV_HEADS, past, LANES), BF16),
                    pltpu.VMEM((N_KV_HEADS, past, LANES), BF16)]
    else:
        out_specs += [pl.BlockSpec((n_seq, 1, L, KV_W), lambda s, l: (s, l, 0, 0))] * 2
        out_shape += [jax.ShapeDtypeStruct((S, DEPTH, L, KV_W), F32)] * 2

    return pl.pallas_call(
        functools.partial(_layers_kernel, L=L, n_seq=n_seq, row_chunk=row_chunk, latent=latent),
        grid=(S // n_seq, DEPTH),
        in_specs=in_specs,
        out_specs=out_specs,
        out_shape=out_shape,
        scratch_shapes=scratch,
        compiler_params=pltpu.CompilerParams(
            dimension_semantics=("arbitrary", "arbitrary"),
            vmem_limit_bytes=VMEM_LIMIT_BYTES),
        name="latent_layers" if latent else "context_layers",
    )(*args)


def _rope_tables(L):
    quarter = HEAD_DIM // 4
    t = jnp.arange(L)
    rows = (t // GRID_W).astype(F32)
    cols = (t % GRID_W).astype(F32)
    freqs = ROPE_BASE ** (-jnp.arange(quarter, dtype=F32) / quarter)
    ang_r = rows[:, None] * freqs[None, :]
    ang_c = cols[:, None] * freqs[None, :]
    cos = jnp.concatenate([jnp.cos(ang_r)] * 2 + [jnp.cos(ang_c)] * 2, axis=-1)
    sin = jnp.concatenate([-jnp.sin(ang_r), jnp.sin(ang_r), -jnp.sin(ang_c), jnp.sin(ang_c)], axis=-1)
    return jnp.tile(cos, (1, 2)), jnp.tile(sin, (1, 2))


def kernel(x_prompt, x_sample, c, cache_k, cache_v, c_ctx, w_ada, b_ada, norm_w, w_in, conv_dw, conv_b,
           conv_ln_g, conv_ln_b, conv_pw, attn_sink, pool_w, pool_scale, w_out, final_norm_w):
    n_ctx, ctx_len, _ = x_prompt.shape
    n_lat, lat_len, _ = x_sample.shape
    past = cache_k.shape[2]
    assert n_lat + 1 <= MOD_ROWS

    cvecs = jnp.zeros((MOD_ROWS, D_MODEL), F32).at[:n_lat].set(c).at[n_lat].set(c_ctx)
    mods = _adaln(cvecs, w_ada, b_ada).reshape(DEPTH, MOD_ROWS, 3, D_MODEL)

    n_groups = len(POOL_WINDOWS)
    eye = jnp.eye(n_groups, dtype=F32)
    pool_bd = (pool_w[:, :, :, None, :] * eye[None, :, None, :, None]).reshape(DEPTH, POOL_W, POOL_W)
    vecs = jnp.zeros((DEPTH, 8, CONV_W), F32)
    vecs = vecs.at[:, 0].set(conv_b).at[:, 1].set(conv_ln_g).at[:, 2].set(conv_ln_b).at[:, 3].set(pool_scale)
    weights = (
        attn_sink,
        norm_w.reshape(DEPTH, 1, D_MODEL),
        w_in.astype(BF16),
        jnp.pad(conv_dw, ((0, 0), (0, CONV_K_PADDED - CONV_K), (0, 0))),
        vecs,
        conv_pw.astype(BF16),
        pool_bd.astype(BF16),
        w_out.astype(BF16),
        final_norm_w.reshape(1, D_MODEL),
    )

    n_seq_ctx = max(d for d in range(1, CTX_SEQS_PER_STEP + 1) if n_ctx % d == 0)
    y_prompt, new_k, new_v = _mixer_layers(x_prompt, mods, lambda s: n_lat, weights, n_seq_ctx)
    new_cache_k = new_k.reshape(n_ctx, DEPTH, ctx_len, N_KV_HEADS, HEAD_DIM)
    new_cache_v = new_v.reshape(n_ctx, DEPTH, ctx_len, N_KV_HEADS, HEAD_DIM)

    (y_sample,) = _mixer_layers(
        x_sample, mods, lambda s: s, weights, 1,
        rope=_rope_tables(lat_len),
        cache=(cache_k.reshape(n_lat, DEPTH, past, KV_W), cache_v.reshape(n_lat, DEPTH, past, KV_W)))
    return (y_prompt, y_sample, new_cache_k, new_cache_v)
```

```python
import functools

import jax
import jax.numpy as jnp
from jax import lax
from jax.experimental import pallas as pl
from jax.experimental.pallas import tpu as pltpu

D_MODEL = 1024
DEPTH = 4
GRID_W = 64
CONV_W = 256
ATTN_W = 512
POOL_W = 256
HEAD_DIM = 64
N_HEADS = 8
N_KV_HEADS = 2
GQA_GROUP = 4
KV_W = 128
WINDOW = 128
BLOCK = 128
CONV_K = 31
POOL_WINDOWS = (2, 4, 8, 16)
POOL_GROUP_W = 64
ROPE_BASE = 10000.0
EPS = 1e-6
NEG_INF = -1e30
LOG2E = 1.4426950408889634

OFF_A = 0
OFF_Q = 3 * CONV_W
OFF_KV = OFF_Q + ATTN_W
OFF_B_GATE = OFF_KV + 2 * KV_W
OFF_C = OFF_B_GATE + ATTN_W
IN_W = OFF_C + 2 * POOL_W

LANES = 128
CONV_K_PADDED = 32
CONV_PAD = 16
POOL_PAD = 8
MAX_ROW_CHUNK = 512
CONV_TILE = 32
ATTN_UNROLL = 4
CTX_SEQS_PER_STEP = 2
ROW_SUMS_ON_MXU = {True: True, False: False}
SOFTMAX_ROWS = 32
MOD_ROWS = 16
VMEM_LIMIT_BYTES = 56 * 1024 * 1024

F32 = jnp.float32
BF16 = jnp.bfloat16


def _silu(x):
    return x * jax.nn.sigmoid(x)


def _lane_iota(shape):
    return lax.broadcasted_iota(jnp.int32, shape, 1)


def _dup_halves(x):
    r = pltpu.roll(x, HEAD_DIM, axis=1)
    lo = _lane_iota(x.shape) < HEAD_DIM
    return jnp.where(lo, x, r), jnp.where(lo, r, x)


def _with_ones(x):
    r = pltpu.roll(x, HEAD_DIM, axis=1)
    lo = _lane_iota(x.shape) < HEAD_DIM
    return jnp.where(lo, x, 1.0), jnp.where(lo, r, 1.0)


def _rope(x, cos, sin):
    quarter = HEAD_DIM // 4
    fwd = pltpu.roll(x, LANES - quarter, axis=1)
    bwd = pltpu.roll(x, quarter, axis=1)
    first = (_lane_iota(x.shape) & (2 * quarter - 1)) < quarter
    return x * cos + jnp.where(first, fwd, bwd) * sin


def _rms(x):
    return x * lax.rsqrt(jnp.mean(x * x, axis=-1, keepdims=True) + EPS)


def _adaln_kernel(c_ref, w_ref, b_ref, o_ref):
    s = _silu(c_ref[...])
    o_ref[0] = jnp.dot(s.astype(BF16), w_ref[0].astype(BF16), preferred_element_type=F32) + b_ref[0]


def _adaln(cvecs, w_ada, b_ada):
    return pl.pallas_call(
        _adaln_kernel,
        grid=(DEPTH,),
        in_specs=[
            pl.BlockSpec((MOD_ROWS, D_MODEL), lambda l: (0, 0)),
            pl.BlockSpec((1, D_MODEL, 3 * D_MODEL), lambda l: (l, 0, 0)),
            pl.BlockSpec((1, 1, 3 * D_MODEL), lambda l: (l, 0, 0)),
        ],
        out_specs=pl.BlockSpec((1, MOD_ROWS, 3 * D_MODEL), lambda l: (l, 0, 0)),
        out_shape=jax.ShapeDtypeStruct((DEPTH, MOD_ROWS, 3 * D_MODEL), F32),
        compiler_params=pltpu.CompilerParams(
            dimension_semantics=("arbitrary",),
            vmem_limit_bytes=VMEM_LIMIT_BYTES),
        name="adaln_mod",
    )(cvecs, w_ada, b_ada.reshape(DEPTH, 1, 3 * D_MODEL))


def _layers_kernel(*refs, L, n_seq, row_chunk, latent):
    if latent:
        (sink_ref, x_ref, mod_ref, nw_ref, win_ref, dw_ref, vec_ref, pw_ref, pbd_ref, wout_ref,
         fnw_ref, cos_ref, sin_ref, ck_ref, cv_ref,
         o_ref,
         gpad_ref, cpad_ref, gate_ref, qm_ref, kk_ref, vv_ref, mix_ref, s_ref, p_ref, es_ref) = refs
    else:
        (sink_ref, x_ref, mod_ref, nw_ref, win_ref, dw_ref, vec_ref, pw_ref, pbd_ref, wout_ref,
         fnw_ref,
         o_ref, nk_ref, nv_ref,
         gpad_ref, cpad_ref, gate_ref, qm_ref, kk_ref, vv_ref, mix_ref, s_ref, p_ref, es_ref) = refs

    l = pl.program_id(1)
    n_chunks = n_seq * L // row_chunk
    kpad = BLOCK if latent else 0
    assert not latent or n_seq == 1

    def locate(r0):
        return divmod(r0, L)

    @pl.when(l == 0)
    def _():
        o_ref[...] = x_ref[...]

    for c in range(CONV_W // LANES):
        for s in range(n_seq):
            gpad_ref[c, s, 0:CONV_PAD, :] = jnp.zeros((CONV_PAD, LANES), F32)
            gpad_ref[c, s, CONV_PAD + L:CONV_PAD + L + CONV_PAD, :] = jnp.zeros((CONV_PAD, LANES), F32)
            cpad_ref[c, s, 0:POOL_PAD, :] = jnp.zeros((POOL_PAD, LANES), F32)
            cpad_ref[c, s, POOL_PAD + L:POOL_PAD + L + POOL_PAD, :] = jnp.zeros((POOL_PAD, LANES), F32)
    if latent:
        for g in range(N_KV_HEADS):
            for ref in (kk_ref, vv_ref):
                ref[g, 0:kpad, :] = jnp.zeros((kpad, LANES), BF16)
                ref[g, kpad + L:kpad + L + kpad, :] = jnp.zeros((kpad, LANES), BF16)

    shift = mod_ref[0, 0, 0:1, :]
    scl = mod_ref[0, 0, 1:2, :]
    gate = mod_ref[0, 0, 2:3, :]
    norm_scale = nw_ref[0] * (1.0 + scl)

    def proj_chunk(c):
        r0 = c * row_chunk
        seq, rs = locate(r0)
        rows = pl.ds(r0, row_chunk)
        seq_rows = pl.ds(rs, row_chunk)
        x = o_ref[seq, seq_rows, :]
        h = _rms(x) * norm_scale + shift
        hb = h.astype(BF16)

        ua = jnp.dot(hb, win_ref[0, :, OFF_A:OFF_Q], preferred_element_type=F32)
        glu = ua[:, 0:CONV_W] * jax.nn.sigmoid(ua[:, CONV_W:2 * CONV_W])
        for c in range(CONV_W // LANES):
            gpad_ref[c, seq, pl.ds(rs + CONV_PAD, row_chunk), :] = glu[:, c * LANES:(c + 1) * LANES]
        gate_ref[rows, 0:CONV_W] = _silu(ua[:, 2 * CONV_W:3 * CONV_W])

        uq = jnp.dot(hb, win_ref[0, :, OFF_Q:OFF_KV], preferred_element_type=F32)
        ukv = jnp.dot(hb, win_ref[0, :, OFF_KV:OFF_B_GATE], preferred_element_type=F32)
        k = ukv[:, 0:KV_W]
        v = ukv[:, KV_W:2 * KV_W]
        if latent:
            cos = cos_ref[seq_rows, :]
            sin = sin_ref[seq_rows, :]
            k = _rope(k, cos, sin)
        else:
            nk_ref[seq, 0, seq_rows, :] = k
            nv_ref[seq, 0, seq_rows, :] = v
        lo = _lane_iota((row_chunk, LANES)) < HEAD_DIM
        scale = HEAD_DIM ** -0.5 * LOG2E
        for p in range(ATTN_W // LANES):
            qp = uq[:, p * LANES:(p + 1) * LANES]
            if latent:
                qp = _rope(qp, cos, sin)
            qp = qp * scale
            g, c = divmod(2 * p, GQA_GROUP)
            qm_ref[g, rows, c * LANES:(c + 1) * LANES] = jnp.where(lo, qp, 0.0).astype(BF16)
            qm_ref[g, rows, (c + 1) * LANES:(c + 2) * LANES] = jnp.where(lo, 0.0, qp).astype(BF16)
        k0, k1 = _dup_halves(k)
        v0, v1 = _with_ones(v) if ROW_SUMS_ON_MXU[latent] else _dup_halves(v)
        krows = pl.ds(r0 + kpad, row_chunk)
        kk_ref[0, krows, :] = k0.astype(BF16)
        kk_ref[1, krows, :] = k1.astype(BF16)
        vv_ref[0, krows, :] = v0.astype(BF16)
        vv_ref[1, krows, :] = v1.astype(BF16)

        ub = jnp.dot(hb, win_ref[0, :, OFF_B_GATE:OFF_C], preferred_element_type=F32)
        gate_ref[rows, CONV_W:CONV_W + ATTN_W] = _silu(ub)

        uc = jnp.dot(hb, win_ref[0, :, OFF_C:IN_W], preferred_element_type=F32)
        for c in range(POOL_W // LANES):
            cpad_ref[c, seq, pl.ds(rs + POOL_PAD, row_chunk), :] = uc[:, c * LANES:(c + 1) * LANES]
        gate_ref[rows, CONV_W + ATTN_W:D_MODEL] = _silu(uc[:, POOL_W:2 * POOL_W])

    conv_b = vec_ref[0, 0:1, :]
    ln_g = vec_ref[0, 1:2, :]
    ln_b = vec_ref[0, 2:3, :]
    pool_scale = vec_ref[0, 3:4, :]

    def conv_rows(r0):
        seq, rs = locate(r0)
        rows = pl.ds(r0, BLOCK)
        pieces = []
        for t in range(BLOCK // CONV_TILE):
            base = rs + t * CONV_TILE + (CONV_PAD - CONV_K // 2)
            halves = []
            for c in range(CONV_W // LANES):
                acc = jnp.zeros((CONV_TILE, LANES), F32)
                for kk in range(CONV_K):
                    acc = acc + (gpad_ref[c, seq, pl.ds(base + kk, CONV_TILE), :]
                                 * dw_ref[0, kk:kk + 1, c * LANES:(c + 1) * LANES])
                halves.append(acc)
            y = jnp.concatenate(halves, axis=1) + conv_b
            mu = jnp.mean(y, axis=-1, keepdims=True)
            yc = y - mu
            var = jnp.mean(yc * yc, axis=-1, keepdims=True)
            yn = (yc * lax.rsqrt(var + EPS)) * ln_g + ln_b
            pieces.append(_silu(yn).astype(BF16))
        z = jnp.concatenate(pieces, axis=0)
        a = jnp.dot(z, pw_ref[0], preferred_element_type=F32)
        mix_ref[rows, 0:CONV_W] = (a * gate_ref[rows, 0:CONV_W]).astype(BF16)

    def pool_rows(r0):
        seq, rs = locate(r0)
        rows = pl.ds(r0, BLOCK)
        t = rs + lax.broadcasted_iota(jnp.int32, (BLOCK, LANES), 0)
        first_group = _lane_iota((BLOCK, LANES)) < POOL_GROUP_W
        deltas = []
        for c in range(POOL_W // LANES):

            def tok(off, c=c):
                return cpad_ref[c, seq, pl.ds(rs + POOL_PAD + off, BLOCK), :]

            centre = tok(0)
            half_a, half_b = POOL_WINDOWS[2 * c] // 2, POOL_WINDOWS[2 * c + 1] // 2
            sum_a = centre
            for off in list(range(-half_a, 0)) + list(range(1, half_a)):
                sum_a = sum_a + tok(off)
            sum_b = sum_a
            for off in list(range(-half_b, -half_a)) + list(range(half_a, half_b)):
                sum_b = sum_b + tok(off)
            half_w = jnp.where(first_group, half_a, half_b)
            cnt = jnp.minimum(t + half_w, L) - jnp.maximum(t - half_w, 0)
            mean = jnp.where(first_group, sum_a, sum_b) / cnt.astype(F32)
            deltas.append((mean - centre).astype(BF16))
        d = jnp.concatenate(deltas, axis=1)
        c_out = jnp.dot(d, pbd_ref[0], preferred_element_type=F32) * pool_scale
        c0 = CONV_W + ATTN_W
        mix_ref[rows, c0:D_MODEL] = (c_out * gate_ref[rows, c0:D_MODEL]).astype(BF16)

    n_blocks = n_seq * L // BLOCK
    seq_blocks = L // BLOCK
    n_local = 3 * BLOCK if latent else L
    lo_q = _lane_iota((BLOCK, LANES)) < HEAD_DIM
    contract_last = (((1,), (1,)), ((), ()))

    def fold(pieces, op):
        acc = None
        for piece in pieces:
            for c in range(piece.shape[1] // LANES):
                slab = piece[:, c * LANES:(c + 1) * LANES]
                acc = slab if acc is None else op(acc, slab)
        return acc

    def block_rows(i):
        if isinstance(i, int):
            q0, k0 = i * BLOCK, (i // seq_blocks) * L
        else:
            q0 = pl.multiple_of(i * BLOCK, BLOCK)
            k0 = pl.multiple_of(lax.div(i, jnp.int32(seq_blocks)) * L, L)
        return q0, (pl.ds(q0, n_local) if latent else pl.ds(k0, n_local))

    def scores(i, g):
        q0, local_rows = block_rows(i)
        qm = jnp.concatenate([qm_ref[g, pl.ds(q0, BLOCK), j * LANES:(j + 1) * LANES]
                              for j in range(GQA_GROUP)], axis=0)
        s_ref[g, :, 0:n_local] = lax.dot_general(qm, kk_ref[g, local_rows, :], contract_last,
                                                 preferred_element_type=F32)
        if latent:
            s_ref[g, :, n_local:] = lax.dot_general(qm, ck_ref[0, 0, g], contract_last,
                                                    preferred_element_type=F32)

    def softmax_pv(i, g):
        q0, local_rows = block_rows(i)
        qrows = pl.ds(q0, BLOCK)
        if latent:
            jq = lax.broadcasted_iota(jnp.int32, (SOFTMAX_ROWS, BLOCK), 0)
            jk = lax.broadcasted_iota(jnp.int32, (SOFTMAX_ROWS, BLOCK), 1)
            prev_off = jnp.where(i > 0, 0, BLOCK)
            next_off = jnp.where(i < n_blocks - 1, 0, BLOCK)
        for j in range(GQA_GROUP):
            sink = sink_ref[l, g * GQA_GROUP + j] * LOG2E
            for r in range(BLOCK // SOFTMAX_ROWS):
                rr = slice(j * BLOCK + r * SOFTMAX_ROWS, j * BLOCK + (r + 1) * SOFTMAX_ROWS)
                sj = s_ref[g, rr, :]
                if latent:
                    prev_ok = jk >= jq + (prev_off + r * SOFTMAX_ROWS)
                    next_ok = jk <= jq + (r * SOFTMAX_ROWS - next_off)
                    pieces = [jnp.where(prev_ok, sj[:, 0:BLOCK], NEG_INF),
                              sj[:, BLOCK:2 * BLOCK],
                              jnp.where(next_ok, sj[:, 2 * BLOCK:3 * BLOCK], NEG_INF),
                              sj[:, n_local:]]
                else:
                    pieces = [sj]
                m = jnp.maximum(jnp.max(fold(pieces, jnp.maximum), axis=-1, keepdims=True), sink)
                e = [jnp.exp2(piece - m) for piece in pieces]
                p_ref[g, rr, :] = jnp.concatenate(e, axis=1).astype(BF16)
                tail = jnp.exp2(sink - m)
                if not ROW_SUMS_ON_MXU[latent]:
                    tail = tail + jnp.sum(fold(e, jnp.add), axis=-1, keepdims=True)
                es_ref[g, rr, :] = jnp.broadcast_to(tail, (SOFTMAX_ROWS, LANES))
        o = jnp.dot(p_ref[g, :, 0:n_local], vv_ref[g, local_rows, :],
                    preferred_element_type=F32)
        if latent:
            o = o + jnp.dot(p_ref[g, :, n_local:], cv_ref[0, 0, g], preferred_element_type=F32)
        pairs = []
        for p in range(GQA_GROUP // 2):
            rows_a = slice((2 * p) * BLOCK, (2 * p + 1) * BLOCK)
            rows_b = slice((2 * p + 1) * BLOCK, (2 * p + 2) * BLOCK)
            o_a = o[rows_a]
            o_b = o[rows_b]
            if ROW_SUMS_ON_MXU[latent]:
                inv_a = 1.0 / (o_a + es_ref[g, rows_a, :])
                inv_b = 1.0 / (o_b + es_ref[g, rows_b, :])
                pairs.append(jnp.where(lo_q, o_a * pltpu.roll(inv_a, HEAD_DIM, axis=1),
                                       pltpu.roll(o_b, HEAD_DIM, axis=1) * inv_b))
            else:
                pairs.append(jnp.where(lo_q, o_a / es_ref[g, rows_a, :], o_b / es_ref[g, rows_b, :]))
        c0 = CONV_W + g * (ATTN_W // N_KV_HEADS)
        cols = slice(c0, c0 + ATTN_W // N_KV_HEADS)
        mix_ref[qrows, cols] = (jnp.concatenate(pairs, axis=1) * gate_ref[qrows, cols]).astype(BF16)

    def attn_block(i, carry):
        scores(i, 1)
        softmax_pv(i, 0)
        scores(jnp.minimum(i + 1, n_blocks - 1), 0)
        softmax_pv(i, 1)
        return carry

    def out_chunk(c):
        seq, rs = locate(c * row_chunk)
        seq_rows = pl.ds(rs, row_chunk)
        y = jnp.dot(mix_ref[pl.ds(c * row_chunk, row_chunk), :], wout_ref[0], preferred_element_type=F32)
        o_ref[seq, seq_rows, :] = o_ref[seq, seq_rows, :] + gate * y

    def vpu_branches(c):
        for b in range(row_chunk // BLOCK):
            conv_rows(c * row_chunk + b * BLOCK)
            pool_rows(c * row_chunk + b * BLOCK)

    for c in range(n_chunks):
        proj_chunk(c)
        if c > 0:
            vpu_branches(c - 1)
    scores(0, 0)
    lax.fori_loop(0, n_blocks, attn_block, 0, unroll=min(n_blocks, ATTN_UNROLL))
    last_blocks = [(n_chunks - 1) * row_chunk + b * BLOCK for b in range(row_chunk // BLOCK)]
    for c in range(n_chunks - 1):
        out_chunk(c)
        if last_blocks:
            r0 = last_blocks.pop(0)
            conv_rows(r0)
            pool_rows(r0)
    for r0 in last_blocks:
        conv_rows(r0)
        pool_rows(r0)
    out_chunk(n_chunks - 1)

    @pl.when(l == DEPTH - 1)
    def _():
        for c in range(n_chunks):
            seq, rs = locate(c * row_chunk)
            seq_rows = pl.ds(rs, row_chunk)
            o_ref[seq, seq_rows, :] = _rms(o_ref[seq, seq_rows, :]) * fnw_ref[...]


def _mixer_layers(x, mods, mod_row, weights, n_seq, rope=None, cache=None):
    S, L, _ = x.shape
    latent = cache is not None
    sink, norm_w, w_in, conv_dw, vecs, conv_pw, pool_bd, w_out, final_norm_w = weights
    kpad = BLOCK if latent else 0
    n_keys = 3 * BLOCK + cache[0].shape[3] if latent else L
    R = n_seq * L
    row_chunk = min(MAX_ROW_CHUNK, L)
    assert S % n_seq == 0 and L % row_chunk == 0 and row_chunk % BLOCK == 0

    per_layer = lambda *tail: (lambda s, l: (l,) + tail)
    in_specs = [
        pl.BlockSpec(memory_space=pltpu.SMEM),
        pl.BlockSpec((n_seq, L, D_MODEL), lambda s, l: (s, 0, 0),
                     pipeline_mode=pl.Buffered(1) if latent else None),
        pl.BlockSpec((1, 1, 3, D_MODEL), lambda s, l: (l, mod_row(s), 0, 0)),
        pl.BlockSpec((1, 1, D_MODEL), per_layer(0, 0)),
        pl.BlockSpec((1, D_MODEL, IN_W), per_layer(0, 0)),
        pl.BlockSpec((1, CONV_K_PADDED, CONV_W), per_layer(0, 0)),
        pl.BlockSpec((1, 8, CONV_W), per_layer(0, 0)),
        pl.BlockSpec((1, CONV_W, CONV_W), per_layer(0, 0)),
        pl.BlockSpec((1, POOL_W, POOL_W), per_layer(0, 0)),
        pl.BlockSpec((1, D_MODEL, D_MODEL), per_layer(0, 0)),
        pl.BlockSpec((1, D_MODEL), lambda s, l: (0, 0)),
    ]
    args = [sink, x, mods, norm_w, w_in, conv_dw, vecs, conv_pw, pool_bd, w_out, final_norm_w]
    out_specs = [pl.BlockSpec((n_seq, L, D_MODEL), lambda s, l: (s, 0, 0))]
    out_shape = [jax.ShapeDtypeStruct((S, L, D_MODEL), F32)]
    scratch = [
        pltpu.VMEM((CONV_W // LANES, n_seq, L + 2 * CONV_PAD, LANES), F32),
        pltpu.VMEM((POOL_W // LANES, n_seq, L + 2 * POOL_PAD, LANES), F32),
        pltpu.VMEM((R, D_MODEL), F32),
        pltpu.VMEM((N_KV_HEADS, R, ATTN_W), BF16),
        pltpu.VMEM((N_KV_HEADS, R + 2 * kpad, LANES), BF16),
        pltpu.VMEM((N_KV_HEADS, R + 2 * kpad, LANES), BF16),
        pltpu.VMEM((R, D_MODEL), BF16),
        pltpu.VMEM((N_KV_HEADS, GQA_GROUP * BLOCK, n_keys), F32),
        pltpu.VMEM((N_KV_HEADS, GQA_GROUP * BLOCK, n_keys), BF16),
        pltpu.VMEM((N_KV_HEADS, GQA_GROUP * BLOCK, LANES), F32),
    ]
    if latent:
        cos, sin = rope
        ck, cv = cache
        past = ck.shape[3]
        in_specs += [
            pl.BlockSpec((L, LANES), lambda s, l: (0, 0)),
            pl.BlockSpec((L, LANES), lambda s, l: (0, 0)),
            pl.BlockSpec((1, 1, N_KV_HEADS, past, LANES), lambda s, l: (s, l, 0, 0, 0)),
            pl.BlockSpec((1, 1, N_KV_HEADS, past, LANES), lambda s, l: (s, l, 0, 0, 0)),
        ]
        args += [cos, sin, ck, cv]
    else:
        out_specs += [pl.BlockSpec((n_seq, 1, L, KV_W), lambda s, l: (s, l, 0, 0))] * 2
        out_shape += [jax.ShapeDtypeStruct((S, DEPTH, L, KV_W), F32)] * 2

    return pl.pallas_call(
        functools.partial(_layers_kernel, L=L, n_seq=n_seq, row_chunk=row_chunk, latent=latent),
        grid=(S // n_seq, DEPTH),
        in_specs=in_specs,
        out_specs=out_specs,
        out_shape=out_shape,
        scratch_shapes=scratch,
        compiler_params=pltpu.CompilerParams(
            dimension_semantics=("arbitrary", "arbitrary"),
            vmem_limit_bytes=VMEM_LIMIT_BYTES),
        name="latent_layers" if latent else "context_layers",
    )(*args)


def _rope_tables(L):
    quarter = HEAD_DIM // 4
    t = jnp.arange(L)
    rows = (t // GRID_W).astype(F32)
    cols = (t % GRID_W).astype(F32)
    freqs = ROPE_BASE ** (-jnp.arange(quarter, dtype=F32) / quarter)
    ang_r = rows[:, None] * freqs[None, :]
    ang_c = cols[:, None] * freqs[None, :]
    cos = jnp.concatenate([jnp.cos(ang_r)] * 2 + [jnp.cos(ang_c)] * 2, axis=-1)
    sin = jnp.concatenate([-jnp.sin(ang_r), jnp.sin(ang_r), -jnp.sin(ang_c), jnp.sin(ang_c)], axis=-1)
    return jnp.tile(cos, (1, 2)), jnp.tile(sin, (1, 2))


def _cache_operand(cache, with_ones):
    h = jnp.moveaxis(cache.astype(BF16), 3, 2)
    return jnp.concatenate([h, jnp.ones_like(h) if with_ones else h], axis=-1)


def kernel(x_prompt, x_sample, c, cache_k, cache_v, c_ctx, w_ada, b_ada, norm_w, w_in, conv_dw, conv_b,
           conv_ln_g, conv_ln_b, conv_pw, attn_sink, pool_w, pool_scale, w_out, final_norm_w):
    n_ctx, ctx_len, _ = x_prompt.shape
    n_lat, lat_len, _ = x_sample.shape
    past = cache_k.shape[2]
    assert n_lat + 1 <= MOD_ROWS

    cvecs = jnp.zeros((MOD_ROWS, D_MODEL), F32).at[:n_lat].set(c).at[n_lat].set(c_ctx)
    mods = _adaln(cvecs, w_ada, b_ada).reshape(DEPTH, MOD_ROWS, 3, D_MODEL)

    n_groups = len(POOL_WINDOWS)
    eye = jnp.eye(n_groups, dtype=F32)
    pool_bd = (pool_w[:, :, :, None, :] * eye[None, :, None, :, None]).reshape(DEPTH, POOL_W, POOL_W)
    vecs = jnp.zeros((DEPTH, 8, CONV_W), F32)
    vecs = vecs.at[:, 0].set(conv_b).at[:, 1].set(conv_ln_g).at[:, 2].set(conv_ln_b).at[:, 3].set(pool_scale)
    weights = (
        attn_sink,
        norm_w.reshape(DEPTH, 1, D_MODEL),
        w_in.astype(BF16),
        jnp.pad(conv_dw, ((0, 0), (0, CONV_K_PADDED - CONV_K), (0, 0))),
        vecs,
        conv_pw.astype(BF16),
        pool_bd.astype(BF16),
        w_out.astype(BF16),
        final_norm_w.reshape(1, D_MODEL),
    )

    n_seq_ctx = max(d for d in range(1, CTX_SEQS_PER_STEP + 1) if n_ctx % d == 0)
    y_prompt, new_k, new_v = _mixer_layers(x_prompt, mods, lambda s: n_lat, weights, n_seq_ctx)
    new_cache_k = new_k.reshape(n_ctx, DEPTH, ctx_len, N_KV_HEADS, HEAD_DIM)
    new_cache_v = new_v.reshape(n_ctx, DEPTH, ctx_len, N_KV_HEADS, HEAD_DIM)

    (y_sample,) = _mixer_layers(
        x_sample, mods, lambda s: s, weights, 1,
        rope=_rope_tables(lat_len),
        cache=(_cache_operand(cache_k, False), _cache_operand(cache_v, ROW_SUMS_ON_MXU[True])))
    return (y_prompt, y_sample, new_cache_k, new_cache_v)
```

```python
import functools

import jax
import jax.numpy as jnp
from jax import lax
from jax.experimental import pallas as pl
from jax.experimental.pallas import tpu as pltpu

D_MODEL = 1024
DEPTH = 4
GRID_W = 64
CONV_W = 256
ATTN_W = 512
POOL_W = 256
HEAD_DIM = 64
N_HEADS = 8
N_KV_HEADS = 2
GQA_GROUP = 4
KV_W = 128
WINDOW = 128
BLOCK = 128
CONV_K = 31
POOL_WINDOWS = (2, 4, 8, 16)
POOL_GROUP_W = 64
ROPE_BASE = 10000.0
EPS = 1e-6
NEG_INF = -1e30
LOG2E = 1.4426950408889634

OFF_A = 0
OFF_Q = 3 * CONV_W
OFF_KV = OFF_Q + ATTN_W
OFF_B_GATE = OFF_KV + 2 * KV_W
OFF_C = OFF_B_GATE + ATTN_W
IN_W = OFF_C + 2 * POOL_W

LANES = 128
CONV_K_PADDED = 32
CONV_PAD = 16
POOL_PAD = 8
MAX_ROW_CHUNK = 512
CONV_TILE = 32
ATTN_UNROLL = 8
CTX_SEQS_PER_STEP = 2
ROW_SUMS_ON_MXU = {True: True, False: False}
SOFTMAX_ROWS = 32
MOD_ROWS = 16
VMEM_LIMIT_BYTES = 56 * 1024 * 1024

F32 = jnp.float32
BF16 = jnp.bfloat16


def _silu(x):
    return x * jax.nn.sigmoid(x)


def _lane_iota(shape):
    return lax.broadcasted_iota(jnp.int32, shape, 1)


def _dup_halves(x):
    r = pltpu.roll(x, HEAD_DIM, axis=1)
    lo = _lane_iota(x.shape) < HEAD_DIM
    return jnp.where(lo, x, r), jnp.where(lo, r, x)


def _with_ones(x):
    r = pltpu.roll(x, HEAD_DIM, axis=1)
    lo = _lane_iota(x.shape) < HEAD_DIM
    return jnp.where(lo, x, 1.0), jnp.where(lo, r, 1.0)


def _rope(x, cos, sin):
    quarter = HEAD_DIM // 4
    fwd = pltpu.roll(x, LANES - quarter, axis=1)
    bwd = pltpu.roll(x, quarter, axis=1)
    first = (_lane_iota(x.shape) & (2 * quarter - 1)) < quarter
    return x * cos + jnp.where(first, fwd, bwd) * sin


def _rms(x):
    return x * lax.rsqrt(jnp.mean(x * x, axis=-1, keepdims=True) + EPS)


def _adaln_kernel(c_ref, w_ref, b_ref, o_ref):
    s = _silu(c_ref[...])
    o_ref[0] = jnp.dot(s.astype(BF16), w_ref[0].astype(BF16), preferred_element_type=F32) + b_ref[0]


def _adaln(cvecs, w_ada, b_ada):
    return pl.pallas_call(
        _adaln_kernel,
        grid=(DEPTH,),
        in_specs=[
            pl.BlockSpec((MOD_ROWS, D_MODEL), lambda l: (0, 0)),
            pl.BlockSpec((1, D_MODEL, 3 * D_MODEL), lambda l: (l, 0, 0)),
            pl.BlockSpec((1, 1, 3 * D_MODEL), lambda l: (l, 0, 0)),
        ],
        out_specs=pl.BlockSpec((1, MOD_ROWS, 3 * D_MODEL), lambda l: (l, 0, 0)),
        out_shape=jax.ShapeDtypeStruct((DEPTH, MOD_ROWS, 3 * D_MODEL), F32),
        compiler_params=pltpu.CompilerParams(
            dimension_semantics=("arbitrary",),
            vmem_limit_bytes=VMEM_LIMIT_BYTES),
        name="adaln_mod",
    )(cvecs, w_ada, b_ada.reshape(DEPTH, 1, 3 * D_MODEL))


def _layers_kernel(*refs, L, n_seq, row_chunk, latent):
    if latent:
        (sink_ref, x_ref, mod_ref, nw_ref, win_ref, dw_ref, vec_ref, pw_ref, pbd_ref, wout_ref,
         fnw_ref, cos_ref, sin_ref, ck_ref, cv_ref,
         o_ref,
         gpad_ref, cpad_ref, gate_ref, qm_ref, kk_ref, vv_ref, mix_ref, s_ref, p_ref, es_ref,
         ckk_ref, cvv_ref) = refs
    else:
        (sink_ref, x_ref, mod_ref, nw_ref, win_ref, dw_ref, vec_ref, pw_ref, pbd_ref, wout_ref,
         fnw_ref,
         o_ref, nk_ref, nv_ref,
         gpad_ref, cpad_ref, gate_ref, qm_ref, kk_ref, vv_ref, mix_ref, s_ref, p_ref, es_ref) = refs

    l = pl.program_id(1)
    n_chunks = n_seq * L // row_chunk
    kpad = BLOCK if latent else 0
    assert not latent or n_seq == 1

    def locate(r0):
        return divmod(r0, L)

    @pl.when(l == 0)
    def _():
        o_ref[...] = x_ref[...]

    for c in range(CONV_W // LANES):
        for s in range(n_seq):
            gpad_ref[c, s, 0:CONV_PAD, :] = jnp.zeros((CONV_PAD, LANES), F32)
            gpad_ref[c, s, CONV_PAD + L:CONV_PAD + L + CONV_PAD, :] = jnp.zeros((CONV_PAD, LANES), F32)
            cpad_ref[c, s, 0:POOL_PAD, :] = jnp.zeros((POOL_PAD, LANES), F32)
            cpad_ref[c, s, POOL_PAD + L:POOL_PAD + L + POOL_PAD, :] = jnp.zeros((POOL_PAD, LANES), F32)
    if latent:
        for g in range(N_KV_HEADS):
            for ref in (kk_ref, vv_ref):
                ref[g, 0:kpad, :] = jnp.zeros((kpad, LANES), BF16)
                ref[g, kpad + L:kpad + L + kpad, :] = jnp.zeros((kpad, LANES), BF16)
        ck0, ck1 = _dup_halves(ck_ref[0, 0])
        cv0, cv1 = _with_ones(cv_ref[0, 0]) if ROW_SUMS_ON_MXU[latent] else _dup_halves(cv_ref[0, 0])
        ckk_ref[0] = ck0.astype(BF16)
        ckk_ref[1] = ck1.astype(BF16)
        cvv_ref[0] = cv0.astype(BF16)
        cvv_ref[1] = cv1.astype(BF16)

    shift = mod_ref[0, 0, 0:1, :]
    scl = mod_ref[0, 0, 1:2, :]
    gate = mod_ref[0, 0, 2:3, :]
    norm_scale = nw_ref[0] * (1.0 + scl)

    def proj_chunk(c):
        r0 = c * row_chunk
        seq, rs = locate(r0)
        rows = pl.ds(r0, row_chunk)
        seq_rows = pl.ds(rs, row_chunk)
        x = o_ref[seq, seq_rows, :]
        h = _rms(x) * norm_scale + shift
        hb = h.astype(BF16)

        ua = jnp.dot(hb, win_ref[0, :, OFF_A:OFF_Q], preferred_element_type=F32)
        glu = ua[:, 0:CONV_W] * jax.nn.sigmoid(ua[:, CONV_W:2 * CONV_W])
        for c in range(CONV_W // LANES):
            gpad_ref[c, seq, pl.ds(rs + CONV_PAD, row_chunk), :] = glu[:, c * LANES:(c + 1) * LANES]
        gate_ref[rows, 0:CONV_W] = _silu(ua[:, 2 * CONV_W:3 * CONV_W])

        uq = jnp.dot(hb, win_ref[0, :, OFF_Q:OFF_KV], preferred_element_type=F32)
        ukv = jnp.dot(hb, win_ref[0, :, OFF_KV:OFF_B_GATE], preferred_element_type=F32)
        k = ukv[:, 0:KV_W]
        v = ukv[:, KV_W:2 * KV_W]
        if latent:
            cos = cos_ref[seq_rows, :]
            sin = sin_ref[seq_rows, :]
            k = _rope(k, cos, sin)
        else:
            nk_ref[seq, 0, seq_rows, :] = k
            nv_ref[seq, 0, seq_rows, :] = v
        lo = _lane_iota((row_chunk, LANES)) < HEAD_DIM
        scale = HEAD_DIM ** -0.5 * LOG2E
        for p in range(ATTN_W // LANES):
            qp = uq[:, p * LANES:(p + 1) * LANES]
            if latent:
                qp = _rope(qp, cos, sin)
            qp = qp * scale
            g, c = divmod(2 * p, GQA_GROUP)
            qm_ref[g, rows, c * LANES:(c + 1) * LANES] = jnp.where(lo, qp, 0.0).astype(BF16)
            qm_ref[g, rows, (c + 1) * LANES:(c + 2) * LANES] = jnp.where(lo, 0.0, qp).astype(BF16)
        k0, k1 = _dup_halves(k)
        v0, v1 = _with_ones(v) if ROW_SUMS_ON_MXU[latent] else _dup_halves(v)
        krows = pl.ds(r0 + kpad, row_chunk)
        kk_ref[0, krows, :] = k0.astype(BF16)
        kk_ref[1, krows, :] = k1.astype(BF16)
        vv_ref[0, krows, :] = v0.astype(BF16)
        vv_ref[1, krows, :] = v1.astype(BF16)

        ub = jnp.dot(hb, win_ref[0, :, OFF_B_GATE:OFF_C], preferred_element_type=F32)
        gate_ref[rows, CONV_W:CONV_W + ATTN_W] = _silu(ub)

        uc = jnp.dot(hb, win_ref[0, :, OFF_C:IN_W], preferred_element_type=F32)
        for c in range(POOL_W // LANES):
            cpad_ref[c, seq, pl.ds(rs + POOL_PAD, row_chunk), :] = uc[:, c * LANES:(c + 1) * LANES]
        gate_ref[rows, CONV_W + ATTN_W:D_MODEL] = _silu(uc[:, POOL_W:2 * POOL_W])

    conv_b = vec_ref[0, 0:1, :]
    ln_g = vec_ref[0, 1:2, :]
    ln_b = vec_ref[0, 2:3, :]
    pool_scale = vec_ref[0, 3:4, :]

    def conv_rows(r0):
        seq, rs = locate(r0)
        rows = pl.ds(r0, BLOCK)
        pieces = []
        for t in range(BLOCK // CONV_TILE):
            base = rs + t * CONV_TILE + (CONV_PAD - CONV_K // 2)
            halves = []
            for c in range(CONV_W // LANES):
                acc = jnp.zeros((CONV_TILE, LANES), F32)
                for kk in range(CONV_K):
                    acc = acc + (gpad_ref[c, seq, pl.ds(base + kk, CONV_TILE), :]
                                 * dw_ref[0, kk:kk + 1, c * LANES:(c + 1) * LANES])
                halves.append(acc)
            y = jnp.concatenate(halves, axis=1) + conv_b
            mu = jnp.mean(y, axis=-1, keepdims=True)
            yc = y - mu
            var = jnp.mean(yc * yc, axis=-1, keepdims=True)
            yn = (yc * lax.rsqrt(var + EPS)) * ln_g + ln_b
            pieces.append(_silu(yn).astype(BF16))
        z = jnp.concatenate(pieces, axis=0)
        a = jnp.dot(z, pw_ref[0], preferred_element_type=F32)
        mix_ref[rows, 0:CONV_W] = (a * gate_ref[rows, 0:CONV_W]).astype(BF16)

    def pool_rows(r0):
        seq, rs = locate(r0)
        rows = pl.ds(r0, BLOCK)
        t = rs + lax.broadcasted_iota(jnp.int32, (BLOCK, LANES), 0)
        first_group = _lane_iota((BLOCK, LANES)) < POOL_GROUP_W
        deltas = []
        for c in range(POOL_W // LANES):

            def tok(off, c=c):
                return cpad_ref[c, seq, pl.ds(rs + POOL_PAD + off, BLOCK), :]

            centre = tok(0)
            half_a, half_b = POOL_WINDOWS[2 * c] // 2, POOL_WINDOWS[2 * c + 1] // 2
            sum_a = centre
            for off in list(range(-half_a, 0)) + list(range(1, half_a)):
                sum_a = sum_a + tok(off)
            sum_b = sum_a
            for off in list(range(-half_b, -half_a)) + list(range(half_a, half_b)):
                sum_b = sum_b + tok(off)
            half_w = jnp.where(first_group, half_a, half_b)
            cnt = jnp.minimum(t + half_w, L) - jnp.maximum(t - half_w, 0)
            mean = jnp.where(first_group, sum_a, sum_b) / cnt.astype(F32)
            deltas.append((mean - centre).astype(BF16))
        d = jnp.concatenate(deltas, axis=1)
        c_out = jnp.dot(d, pbd_ref[0], preferred_element_type=F32) * pool_scale
        c0 = CONV_W + ATTN_W
        mix_ref[rows, c0:D_MODEL] = (c_out * gate_ref[rows, c0:D_MODEL]).astype(BF16)

    n_blocks = n_seq * L // BLOCK
    seq_blocks = L // BLOCK
    n_local = 3 * BLOCK if latent else L
    lo_q = _lane_iota((BLOCK, LANES)) < HEAD_DIM
    contract_last = (((1,), (1,)), ((), ()))

    def fold(pieces, op):
        acc = None
        for piece in pieces:
            for c in range(piece.shape[1] // LANES):
                slab = piece[:, c * LANES:(c + 1) * LANES]
                acc = slab if acc is None else op(acc, slab)
        return acc

    def block_rows(i):
        if isinstance(i, int):
            q0, k0 = i * BLOCK, (i // seq_blocks) * L
        else:
            q0 = pl.multiple_of(i * BLOCK, BLOCK)
            k0 = pl.multiple_of(lax.div(i, jnp.int32(seq_blocks)) * L, L)
        return q0, (pl.ds(q0, n_local) if latent else pl.ds(k0, n_local))

    def scores(i, g):
        q0, local_rows = block_rows(i)
        qm = jnp.concatenate([qm_ref[g, pl.ds(q0, BLOCK), j * LANES:(j + 1) * LANES]
                              for j in range(GQA_GROUP)], axis=0)
        s_ref[g, :, 0:n_local] = lax.dot_general(qm, kk_ref[g, local_rows, :], contract_last,
                                                 preferred_element_type=F32)
        if latent:
            s_ref[g, :, n_local:] = lax.dot_general(qm, ckk_ref[g], contract_last,
                                                    preferred_element_type=F32)

    def softmax_pv(i, g):
        q0, local_rows = block_rows(i)
        qrows = pl.ds(q0, BLOCK)
        if latent:
            jq = lax.broadcasted_iota(jnp.int32, (SOFTMAX_ROWS, BLOCK), 0)
            jk = lax.broadcasted_iota(jnp.int32, (SOFTMAX_ROWS, BLOCK), 1)
            prev_off = jnp.where(i > 0, 0, BLOCK)
            next_off = jnp.where(i < n_blocks - 1, 0, BLOCK)
        for j in range(GQA_GROUP):
            sink = sink_ref[l, g * GQA_GROUP + j] * LOG2E
            for r in range(BLOCK // SOFTMAX_ROWS):
                rr = slice(j * BLOCK + r * SOFTMAX_ROWS, j * BLOCK + (r + 1) * SOFTMAX_ROWS)
                sj = s_ref[g, rr, :]
                if latent:
                    prev_ok = jk >= jq + (prev_off + r * SOFTMAX_ROWS)
                    next_ok = jk <= jq + (r * SOFTMAX_ROWS - next_off)
                    pieces = [jnp.where(prev_ok, sj[:, 0:BLOCK], NEG_INF),
                              sj[:, BLOCK:2 * BLOCK],
                              jnp.where(next_ok, sj[:, 2 * BLOCK:3 * BLOCK], NEG_INF),
                              sj[:, n_local:]]
                else:
                    pieces = [sj]
                m = jnp.maximum(jnp.max(fold(pieces, jnp.maximum), axis=-1, keepdims=True), sink)
                e = [jnp.exp2(piece - m) for piece in pieces]
                p_ref[g, rr, :] = jnp.concatenate(e, axis=1).astype(BF16)
                tail = jnp.exp2(sink - m)
                if not ROW_SUMS_ON_MXU[latent]:
                    tail = tail + jnp.sum(fold(e, jnp.add), axis=-1, keepdims=True)
                es_ref[g, rr, :] = jnp.broadcast_to(tail, (SOFTMAX_ROWS, LANES))
        o = jnp.dot(p_ref[g, :, 0:n_local], vv_ref[g, local_rows, :],
                    preferred_element_type=F32)
        if latent:
            o = o + jnp.dot(p_ref[g, :, n_local:], cvv_ref[g], preferred_element_type=F32)
        pairs = []
        for p in range(GQA_GROUP // 2):
            rows_a = slice((2 * p) * BLOCK, (2 * p + 1) * BLOCK)
            rows_b = slice((2 * p + 1) * BLOCK, (2 * p + 2) * BLOCK)
            o_a = o[rows_a]
            o_b = o[rows_b]
            if ROW_SUMS_ON_MXU[latent]:
                inv_a = 1.0 / (o_a + es_ref[g, rows_a, :])
                inv_b = 1.0 / (o_b + es_ref[g, rows_b, :])
                pairs.append(jnp.where(lo_q, o_a * pltpu.roll(inv_a, HEAD_DIM, axis=1),
                                       pltpu.roll(o_b, HEAD_DIM, axis=1) * inv_b))
            else:
                pairs.append(jnp.where(lo_q, o_a / es_ref[g, rows_a, :], o_b / es_ref[g, rows_b, :]))
        c0 = CONV_W + g * (ATTN_W // N_KV_HEADS)
        cols = slice(c0, c0 + ATTN_W // N_KV_HEADS)
        mix_ref[qrows, cols] = (jnp.concatenate(pairs, axis=1) * gate_ref[qrows, cols]).astype(BF16)

    def attn_block(i, carry):
        scores(i, 1)
        softmax_pv(i, 0)
        scores(jnp.minimum(i + 1, n_blocks - 1), 0)
        softmax_pv(i, 1)
        return carry

    def out_chunk(c):
        seq, rs = locate(c * row_chunk)
        seq_rows = pl.ds(rs, row_chunk)
        y = jnp.dot(mix_ref[pl.ds(c * row_chunk, row_chunk), :], wout_ref[0], preferred_element_type=F32)
        o_ref[seq, seq_rows, :] = o_ref[seq, seq_rows, :] + gate * y

    def vpu_branches(c):
        for b in range(row_chunk // BLOCK):
            conv_rows(c * row_chunk + b * BLOCK)
            pool_rows(c * row_chunk + b * BLOCK)

    for c in range(n_chunks):
        proj_chunk(c)
        if c > 0:
            vpu_branches(c - 1)
    scores(0, 0)
    lax.fori_loop(0, n_blocks, attn_block, 0, unroll=min(n_blocks, ATTN_UNROLL))
    last_blocks = [(n_chunks - 1) * row_chunk + b * BLOCK for b in range(row_chunk // BLOCK)]
    for c in range(n_chunks - 1):
        out_chunk(c)
        if last_blocks:
            r0 = last_blocks.pop(0)
            conv_rows(r0)
            pool_rows(r0)
    for r0 in last_blocks:
        conv_rows(r0)
        pool_rows(r0)
    out_chunk(n_chunks - 1)

    @pl.when(l == DEPTH - 1)
    def _():
        for c in range(n_chunks):
            seq, rs = locate(c * row_chunk)
            seq_rows = pl.ds(rs, row_chunk)
            o_ref[seq, seq_rows, :] = _rms(o_ref[seq, seq_rows, :]) * fnw_ref[...]


def _mixer_layers(x, mods, mod_row, weights, n_seq, rope=None, cache=None):
    S, L, _ = x.shape
    latent = cache is not None
    sink, norm_w, w_in, conv_dw, vecs, conv_pw, pool_bd, w_out, final_norm_w = weights
    kpad = BLOCK if latent else 0
    n_keys = 3 * BLOCK + cache[0].shape[2] if latent else L
    R = n_seq * L
    row_chunk = min(MAX_ROW_CHUNK, L)
    assert S % n_seq == 0 and L % row_chunk == 0 and row_chunk % BLOCK == 0

    per_layer = lambda *tail: (lambda s, l: (l,) + tail)
    in_specs = [
        pl.BlockSpec(memory_space=pltpu.SMEM),
        pl.BlockSpec((n_seq, L, D_MODEL), lambda s, l: (s, 0, 0),
                     pipeline_mode=pl.Buffered(1) if latent else None),
        pl.BlockSpec((1, 1, 3, D_MODEL), lambda s, l: (l, mod_row(s), 0, 0)),
        pl.BlockSpec((1, 1, D_MODEL), per_layer(0, 0)),
        pl.BlockSpec((1, D_MODEL, IN_W), per_layer(0, 0)),
        pl.BlockSpec((1, CONV_K_PADDED, CONV_W), per_layer(0, 0)),
        pl.BlockSpec((1, 8, CONV_W), per_layer(0, 0)),
        pl.BlockSpec((1, CONV_W, CONV_W), per_layer(0, 0)),
        pl.BlockSpec((1, POOL_W, POOL_W), per_layer(0, 0)),
        pl.BlockSpec((1, D_MODEL, D_MODEL), per_layer(0, 0)),
        pl.BlockSpec((1, D_MODEL), lambda s, l: (0, 0)),
    ]
    args = [sink, x, mods, norm_w, w_in, conv_dw, vecs, conv_pw, pool_bd, w_out, final_norm_w]
    out_specs = [pl.BlockSpec((n_seq, L, D_MODEL), lambda s, l: (s, 0, 0))]
    out_shape = [jax.ShapeDtypeStruct((S, L, D_MODEL), F32)]
    scratch = [
        pltpu.VMEM((CONV_W // LANES, n_seq, L + 2 * CONV_PAD, LANES), F32),
        pltpu.VMEM((POOL_W // LANES, n_seq, L + 2 * POOL_PAD, LANES), F32),
        pltpu.VMEM((R, D_MODEL), F32),
        pltpu.VMEM((N_KV_HEADS, R, ATTN_W), BF16),
        pltpu.VMEM((N_KV_HEADS, R + 2 * kpad, LANES), BF16),
        pltpu.VMEM((N_KV_HEADS, R + 2 * kpad, LANES), BF16),
        pltpu.VMEM((R, D_MODEL), BF16),
        pltpu.VMEM((N_KV_HEADS, GQA_GROUP * BLOCK, n_keys), F32),
        pltpu.VMEM((N_KV_HEADS, GQA_GROUP * BLOCK, n_keys), BF16),
        pltpu.VMEM((N_KV_HEADS, GQA_GROUP * BLOCK, LANES), F32),
    ]
    if latent:
        cos, sin = rope
        ck, cv = cache
        past = ck.shape[2]
        in_specs += [
            pl.BlockSpec((L, LANES), lambda s, l: (0, 0)),
            pl.BlockSpec((L, LANES), lambda s, l: (0, 0)),
            pl.BlockSpec((1, 1, past, KV_W), lambda s, l: (s, l, 0, 0)),
            pl.BlockSpec((1, 1, past, KV_W), lambda s, l: (s, l, 0, 0)),
        ]
        args += [cos, sin, ck, cv]
        scratch += [pltpu.VMEM((N_KV_HEADS, past, LANES), BF16),
                    pltpu.VMEM((N_KV_HEADS, past, LANES), BF16)]
    else:
        out_specs += [pl.BlockSpec((n_seq, 1, L, KV_W), lambda s, l: (s, l, 0, 0))] * 2
        out_shape += [jax.ShapeDtypeStruct((S, DEPTH, L, KV_W), F32)] * 2

    return pl.pallas_call(
        functools.partial(_layers_kernel, L=L, n_seq=n_seq, row_chunk=row_chunk, latent=latent),
        grid=(S // n_seq, DEPTH),
        in_specs=in_specs,
        out_specs=out_specs,
        out_shape=out_shape,
        scratch_shapes=scratch,
        compiler_params=pltpu.CompilerParams(
            dimension_semantics=("arbitrary", "arbitrary"),
            vmem_limit_bytes=VMEM_LIMIT_BYTES),
        name="latent_layers" if latent else "context_layers",
    )(*args)


def _rope_tables(L):
    quarter = HEAD_DIM // 4
    t = jnp.arange(L)
    rows = (t // GRID_W).astype(F32)
    cols = (t % GRID_W).astype(F32)
    freqs = ROPE_BASE ** (-jnp.arange(quarter, dtype=F32) / quarter)
    ang_r = rows[:, None] * freqs[None, :]
    ang_c = cols[:, None] * freqs[None, :]
    cos = jnp.concatenate([jnp.cos(ang_r)] * 2 + [jnp.cos(ang_c)] * 2, axis=-1)
    sin = jnp.concatenate([-jnp.sin(ang_r), jnp.sin(ang_r), -jnp.sin(ang_c), jnp.sin(ang_c)], axis=-1)
    return jnp.tile(cos, (1, 2)), jnp.tile(sin, (1, 2))


def kernel(x_prompt, x_sample, c, cache_k, cache_v, c_ctx, w_ada, b_ada, norm_w, w_in, conv_dw, conv_b,
           conv_ln_g, conv_ln_b, conv_pw, attn_sink, pool_w, pool_scale, w_out, final_norm_w):
    n_ctx, ctx_len, _ = x_prompt.shape
    n_lat, lat_len, _ = x_sample.shape
    past = cache_k.shape[2]
    assert n_lat + 1 <= MOD_ROWS

    cvecs = jnp.zeros((MOD_ROWS, D_MODEL), F32).at[:n_lat].set(c).at[n_lat].set(c_ctx)
    mods = _adaln(cvecs, w_ada, b_ada).reshape(DEPTH, MOD_ROWS, 3, D_MODEL)

    n_groups = len(POOL_WINDOWS)
    eye = jnp.eye(n_groups, dtype=F32)
    pool_bd = (pool_w[:, :, :, None, :] * eye[None, :, None, :, None]).reshape(DEPTH, POOL_W, POOL_W)
    vecs = jnp.zeros((DEPTH, 8, CONV_W), F32)
    vecs = vecs.at[:, 0].set(conv_b).at[:, 1].set(conv_ln_g).at[:, 2].set(conv_ln_b).at[:, 3].set(pool_scale)
    weights = (
        attn_sink,
        norm_w.reshape(DEPTH, 1, D_MODEL),
        w_in.astype(BF16),
        jnp.pad(conv_dw, ((0, 0), (0, CONV_K_PADDED - CONV_K), (0, 0))),
        vecs,
        conv_pw.astype(BF16),
        pool_bd.astype(BF16),
        w_out.astype(BF16),
        final_norm_w.reshape(1, D_MODEL),
    )

    n_seq_ctx = max(d for d in range(1, CTX_SEQS_PER_STEP + 1) if n_ctx % d == 0)
    y_prompt, new_k, new_v = _mixer_layers(x_prompt, mods, lambda s: n_lat, weights, n_seq_ctx)
    new_cache_k = new_k.reshape(n_ctx, DEPTH, ctx_len, N_KV_HEADS, HEAD_DIM)
    new_cache_v = new_v.reshape(n_ctx, DEPTH, ctx_len, N_KV_HEADS, HEAD_DIM)

    (y_sample,) = _mixer_layers(
        x_sample, mods, lambda s: s, weights, 1,
        rope=_rope_tables(lat_len),
        cache=(cache_k.reshape(n_lat, DEPTH, past, KV_W), cache_v.reshape(n_lat, DEPTH, past, KV_W)))
    return (y_prompt, y_sample, new_cache_k, new_cache_v)
```
